```python
import jax, jax.numpy as jnp
from jax import lax
import numpy as np

D_MODEL = 1024
BATCH = 1
SEQ = 16384
DEPTH = 4

N_MEM = 256
BRANCH_WIDTH = 512
N_BRANCH = 3
MLA_HEADS = 8
MLA_NOPE = 64
MLA_ROPE = 32
MLA_QK = MLA_NOPE + MLA_ROPE
MLA_V = BRANCH_WIDTH // MLA_HEADS
Q_RANK = 384
KV_RANK = 256
ROPE_THETA = 10000.0
Q_BLOCK = 128
SG_CHUNK = 128
SG_GROUPS = 4
SG_WIDTH = BRANCH_WIDTH
X_HEADS = 4
X_HEAD_DIM = BRANCH_WIDTH // X_HEADS
D_FF = 2688
N_EXPERTS = 8
TOP_K = 2
D_FF_EXPERT = 3584
N_DENSE = (DEPTH + 1) // 2
N_MOE = DEPTH // 2
EPS = 1e-6

P_DQ = Q_RANK
P_DKV = KV_RANK
P_KR = MLA_ROPE
P_SG = 2 * SG_WIDTH
P_XQ = X_HEADS * X_HEAD_DIM
P_GATE = N_BRANCH * D_MODEL
P_TOTAL = P_DQ + P_DKV + P_KR + P_SG + P_XQ + P_GATE
SPLITS = [P_DQ, P_DQ + P_DKV, P_DQ + P_DKV + P_KR, P_DQ + P_DKV + P_KR + P_SG,
          P_DQ + P_DKV + P_KR + P_SG + P_XQ]

kernel_name = 'hybrid_mla_sgmlp_mem_moe_encoder'


def rmsnorm(x, g):
    xf = x.astype(jnp.float32)
    y = xf * lax.rsqrt(jnp.mean(xf * xf, axis=-1, keepdims=True) + EPS)
    return (y * g.astype(jnp.float32)).astype(x.dtype)


def rope(t, positions):
    r = t.shape[-1]
    inv_freq = 1.0 / (ROPE_THETA ** (jnp.arange(0, r, 2, dtype=jnp.float32) / r))
    ang = positions.astype(jnp.float32)[..., None] * inv_freq
    cos = jnp.cos(ang)[:, :, None, :]
    sin = jnp.sin(ang)[:, :, None, :]
    tf = t.astype(jnp.float32)
    t1, t2 = tf[..., : r // 2], tf[..., r // 2:]
    out = jnp.concatenate([t1 * cos - t2 * sin, t1 * sin + t2 * cos], axis=-1)
    return out.astype(t.dtype)


def mla_branch(c_q_raw, c_kv_raw, k_r_raw, positions, g_q, g_kv, w_uq, w_uk, w_uv):
    b, s, _ = c_q_raw.shape
    c_q = rmsnorm(c_q_raw, g_q)
    q = (c_q @ w_uq).reshape(b, s, MLA_HEADS, MLA_QK)
    q = jnp.concatenate([q[..., :MLA_NOPE], rope(q[..., MLA_NOPE:], positions)], axis=-1)
    q = q * (MLA_QK ** -0.5)
    c_kv = rmsnorm(c_kv_raw, g_kv)
    k_nope = (c_kv @ w_uk).reshape(b, s, MLA_HEADS, MLA_NOPE)
    v = (c_kv @ w_uv).reshape(b, s, MLA_HEADS, MLA_V)
    k_rope = rope(k_r_raw[:, :, None, :], positions)
    k = jnp.concatenate([k_nope, jnp.broadcast_to(k_rope, (b, s, MLA_HEADS, MLA_ROPE))], axis=-1)
    qb = q.reshape(b, s // Q_BLOCK, Q_BLOCK, MLA_HEADS, MLA_QK).transpose(1, 0, 2, 3, 4)

    def attend(q_blk):
        sc = jnp.einsum('bqhd,bkhd->bhqk', q_blk, k).astype(jnp.float32)
        pr = jax.nn.softmax(sc, axis=-1).astype(v.dtype)
        return jnp.einsum('bhqk,bkhd->bqhd', pr, v)

    o = lax.map(attend, qb)
    return o.transpose(1, 0, 2, 3, 4).reshape(b, s, MLA_HEADS * MLA_V)


def spatial_gating_branch(z, g_v, w_s, b_s):
    b, s, _ = z.shape
    z = jax.nn.gelu(z)
    u, v = z[..., :SG_WIDTH], z[..., SG_WIDTH:]
    v = rmsnorm(v, g_v)
    vc = v.reshape(b, s // SG_CHUNK, SG_CHUNK, SG_GROUPS, SG_WIDTH // SG_GROUPS)
    sg = jnp.einsum('gij,bnjgc->bnigc', w_s, vc) + b_s.T[None, None, :, :, None]
    return u * sg.reshape(b, s, SG_WIDTH)


def memory_branch(q_raw, mem_n, w_mkv):
    b, s, _ = q_raw.shape
    q = q_raw.reshape(b, s, X_HEADS, X_HEAD_DIM) * (X_HEAD_DIM ** -0.5)
    kv = mem_n @ w_mkv
    m = kv.shape[1]
    k = kv[..., :BRANCH_WIDTH].reshape(b, m, X_HEADS, X_HEAD_DIM)
    v = kv[..., BRANCH_WIDTH:].reshape(b, m, X_HEADS, X_HEAD_DIM)
    sc = jnp.einsum('bshd,bmhd->bhsm', q, k).astype(jnp.float32)
    pr = jax.nn.softmax(sc, axis=-1).astype(v.dtype)
    return jnp.einsum('bhsm,bmhd->bshd', pr, v).reshape(b, s, BRANCH_WIDTH)


def swiglu(h, w1, w3, w2):
    return (jax.nn.silu(h @ w1) * (h @ w3)) @ w2


def moe_swiglu(h, w_router, w1, w3, w2):
    logits = (h @ w_router).astype(jnp.float32)
    top_v, top_i = lax.top_k(logits, TOP_K)
    wts = jax.nn.softmax(top_v, axis=-1)
    combine = jnp.sum(jax.nn.one_hot(top_i, N_EXPERTS, dtype=jnp.float32) * wts[..., None], axis=-2)
    combine = combine.astype(h.dtype)
    out = jnp.zeros_like(h)
    for e in range(N_EXPERTS):
        out = out + combine[..., e:e + 1] * swiglu(h, w1[e], w3[e], w2[e])
    return out


def setup_inputs(seed: int = 0) -> dict:
    key = jax.random.key(seed)
    ks = iter(jax.random.split(key, 40))
    f32 = jnp.float32

    def nrm(shape, fan_in):
        return jax.random.normal(next(ks), shape, f32) * (fan_in ** -0.5)

    def gain(shape):
        return 1.0 + 0.02 * jax.random.normal(next(ks), shape, f32)

    x = jax.random.normal(next(ks), (BATCH, SEQ, D_MODEL), f32)
    mem = jax.random.normal(next(ks), (BATCH, N_MEM, D_MODEL), f32)
    positions = jnp.broadcast_to(jnp.arange(SEQ, dtype=jnp.int32), (BATCH, SEQ))
    return {
        'x': x,
        'mem': mem,
        'positions': positions,
        'g_mix': gain((DEPTH, D_MODEL)),
        'w_in': nrm((DEPTH, D_MODEL, P_TOTAL), D_MODEL),
        'g_q': gain((DEPTH, Q_RANK)),
        'g_kv': gain((DEPTH, KV_RANK)),
        'w_uq': nrm((DEPTH, Q_RANK, MLA_HEADS * MLA_QK), Q_RANK),
        'w_uk': nrm((DEPTH, KV_RANK, MLA_HEADS * MLA_NOPE), KV_RANK),
        'w_uv': nrm((DEPTH, KV_RANK, MLA_HEADS * MLA_V), KV_RANK),
        'g_sg': gain((DEPTH, SG_WIDTH)),
        'w_s': nrm((DEPTH, SG_GROUPS, SG_CHUNK, SG_CHUNK), SG_CHUNK),
        'b_s': gain((DEPTH, SG_GROUPS, SG_CHUNK)),
        'g_mem': gain((D_MODEL,)),
        'w_mkv': nrm((DEPTH, D_MODEL, 2 * BRANCH_WIDTH), D_MODEL),
        'w_br': nrm((DEPTH, N_BRANCH, BRANCH_WIDTH, D_MODEL), BRANCH_WIDTH),
        'b_gate': 0.02 * jax.random.normal(next(ks), (DEPTH, N_BRANCH, D_MODEL), f32),
        'w_out': nrm((DEPTH, D_MODEL, D_MODEL), D_MODEL),
        'g_ffn': gain((DEPTH, D_MODEL)),
        'dense_w1': nrm((N_DENSE, D_MODEL, D_FF), D_MODEL),
        'dense_w3': nrm((N_DENSE, D_MODEL, D_FF), D_MODEL),
        'dense_w2': nrm((N_DENSE, D_FF, D_MODEL), D_FF),
        'w_router': nrm((N_MOE, D_MODEL, N_EXPERTS), D_MODEL),
        'moe_w1': nrm((N_MOE, N_EXPERTS, D_MODEL, D_FF_EXPERT), D_MODEL),
        'moe_w3': nrm((N_MOE, N_EXPERTS, D_MODEL, D_FF_EXPERT), D_MODEL),
        'moe_w2': nrm((N_MOE, N_EXPERTS, D_FF_EXPERT, D_MODEL), D_FF_EXPERT),
        'g_final': gain((D_MODEL,)),
    }


def reference(x, mem, positions, g_mix, w_in, g_q, g_kv, w_uq, w_uk, w_uv, g_sg, w_s, b_s,
              g_mem, w_mkv, w_br, b_gate, w_out, g_ffn, dense_w1, dense_w3, dense_w2,
              w_router, moe_w1, moe_w3, moe_w2, g_final):
    b, s, d = x.shape
    mem_n = rmsnorm(mem, g_mem)
    for l in range(DEPTH):
        h = rmsnorm(x, g_mix[l])
        p = h @ w_in[l]
        c_q, c_kv, k_r, z_sg, q_x, gate_logits = jnp.split(p, SPLITS, axis=-1)
        y_a = mla_branch(c_q, c_kv, k_r, positions, g_q[l], g_kv[l], w_uq[l], w_uk[l], w_uv[l])
        y_b = spatial_gating_branch(z_sg, g_sg[l], w_s[l], b_s[l])
        y_c = memory_branch(q_x, mem_n, w_mkv[l])
        ys = jnp.stack([y_a, y_b, y_c], axis=2)
        yd = jnp.einsum('bsnw,nwd->bsnd', ys, w_br[l])
        gates = jax.nn.sigmoid(gate_logits.reshape(b, s, N_BRANCH, d) + b_gate[l])
        x = x + jnp.sum(gates * yd, axis=2) @ w_out[l]
        h = rmsnorm(x, g_ffn[l])
        if l % 2 == 0:
            j = l // 2
            x = x + swiglu(h, dense_w1[j], dense_w3[j], dense_w2[j])
        else:
            j = l // 2
            x = x + moe_swiglu(h, w_router[j], moe_w1[j], moe_w3[j], moe_w2[j])
    return rmsnorm(x, g_final)
```

```python
import functools

import jax
import jax.numpy as jnp
from jax import lax
from jax.experimental import pallas as pl
from jax.experimental.pallas import tpu as pltpu

D_MODEL = 1024
N_BRANCH = 3
BRANCH_WIDTH = 512
MLA_HEADS = 8
MLA_NOPE = 64
MLA_ROPE = 32
MLA_QK = MLA_NOPE + MLA_ROPE
MLA_V = BRANCH_WIDTH // MLA_HEADS
Q_RANK = 384
KV_RANK = 256
ROPE_THETA = 10000.0
SG_CHUNK = 128
SG_GROUPS = 4
SG_WIDTH = BRANCH_WIDTH
X_HEADS = 4
X_HEAD_DIM = BRANCH_WIDTH // X_HEADS
N_EXPERTS = 8
EPS = 1e-6

LANE = 128
HEAD_PAD = LANE
QK_PAD = MLA_HEADS * HEAD_PAD
VMEM_LIMIT = 56 * 1024 * 1024

F32 = jnp.float32
BF16 = jnp.bfloat16


def _dot(a, b):
    return jnp.dot(a, b, preferred_element_type=F32)


def _dot_nt(a, b):
    return lax.dot_general(a, b, (((1,), (1,)), ((), ())), preferred_element_type=F32)


def _rms(x, g):
    return x * lax.rsqrt(jnp.mean(x * x, axis=-1, keepdims=True) + EPS) * g


def _const_spec(shape):
    nd = len(shape)
    return pl.BlockSpec(shape, lambda *_: (0,) * nd, pipeline_mode=pl.Buffered(1))


def _row_spec(tm, width):
    return pl.BlockSpec((tm, width), lambda i: (i, 0))


def _params(*sem):
    return pltpu.CompilerParams(dimension_semantics=sem, vmem_limit_bytes=VMEM_LIMIT)


def _mem_kv_kernel(mem_ref, g_ref, w_ref, o_ref):
    mem_n = _rms(mem_ref[...], g_ref[...]).astype(BF16)
    o_ref[...] = _dot(mem_n, w_ref[...]).astype(BF16)


def _mem_kv(mem, g_mem, w_mkv):
    m = mem.shape[0]
    return pl.pallas_call(
        _mem_kv_kernel,
        out_shape=jax.ShapeDtypeStruct((m, 2 * BRANCH_WIDTH), BF16),
        name="mem_kv",
    )(mem, g_mem, w_mkv)


def _mixer_pre_kernel(x_ref, pos_ref, gmix_ref, wdq_ref, wdkv_ref, wkr_ref, wsg_ref, wxq_ref,
                      gq_ref, gkv_ref, wuq_ref, wuqr_ref, wuk_ref, wuv_ref, invf_ref, vone_ref,
                      gsg_ref, ws_ref, bst_ref, kmt_ref, vm_ref,
                      q_out, k_out, v_out, yb_out, yc_out):
    tm = x_ref.shape[0]
    h = _rms(x_ref[...], gmix_ref[...]).astype(BF16)

    ang = pos_ref[...] * invf_ref[...]
    cs = jnp.cos(ang)
    sn = jnp.sin(ang)

    cq = _rms(_dot(h, wdq_ref[...]), gq_ref[...]).astype(BF16)
    qa = _dot(cq, wuq_ref[...])
    qb = _dot(cq, wuqr_ref[...])
    ckv = _rms(_dot(h, wdkv_ref[...]), gkv_ref[...]).astype(BF16)
    kn = _dot(ckv, wuk_ref[...])
    kr = _dot(h, wkr_ref[...])
    k_rope = kr[:, :LANE] * cs + kr[:, LANE:] * sn
    scale = MLA_QK ** -0.5
    for hd in range(MLA_HEADS):
        sl = slice(hd * HEAD_PAD, (hd + 1) * HEAD_PAD)
        q_out[:, sl] = ((qa[:, sl] * cs + qb[:, sl] * sn) * scale).astype(BF16)
        k_out[:, sl] = (kn[:, sl] + k_rope).astype(BF16)
    v_out[...] = (_dot(ckv, wuv_ref[...]) + vone_ref[...]).astype(BF16)

    z = jax.nn.gelu(_dot(h, wsg_ref[...]))
    u = z[:, :SG_WIDTH]
    vn = _rms(z[:, SG_WIDTH:], gsg_ref[...]).astype(BF16)
    gw = SG_WIDTH // SG_GROUPS
    for n in range(tm // SG_CHUNK):
        rows = slice(n * SG_CHUNK, (n + 1) * SG_CHUNK)
        for g in range(SG_GROUPS):
            cols = slice(g * gw, (g + 1) * gw)
            sg = _dot(ws_ref[g], vn[rows, cols]) + bst_ref[:, g:g + 1]
            yb_out[rows, cols] = (u[rows, cols] * sg).astype(BF16)

    qx = _dot(h, wxq_ref[...]) * (X_HEAD_DIM ** -0.5)
    for hd in range(X_HEADS):
        sl = slice(hd * X_HEAD_DIM, (hd + 1) * X_HEAD_DIM)
        sc = _dot(qx[:, sl].astype(BF16), kmt_ref[sl, :])
        p = jnp.exp(sc - jnp.max(sc, axis=-1, keepdims=True))
        l = jnp.sum(p, axis=-1, keepdims=True)
        o = _dot(p.astype(BF16), vm_ref[:, sl])
        yc_out[:, sl] = (o / l).astype(BF16)


def _mixer_pre(x, pos, lw, tm):
    s = x.shape[0]
    ins = [x, pos, lw["g_mix"], lw["w_dq"], lw["w_dkv"], lw["w_kr"], lw["w_sg"], lw["w_xq"],
           lw["g_q"], lw["g_kv"], lw["w_uq"], lw["w_uqr"], lw["w_uk"], lw["w_uv"],
           lw["inv_freq"], lw["v_one"], lw["g_sg"], lw["w_s"], lw["b_st"], lw["k_mem_t"],
           lw["v_mem"]]
    in_specs = [_row_spec(tm, D_MODEL), _row_spec(tm, 1)] + [_const_spec(a.shape) for a in ins[2:]]
    out_shape = [jax.ShapeDtypeStruct((s, QK_PAD), BF16)] * 3 + \
                [jax.ShapeDtypeStruct((s, BRANCH_WIDTH), BF16)] * 2
    out_specs = [_row_spec(tm, QK_PAD)] * 3 + [_row_spec(tm, BRANCH_WIDTH)] * 2
    return pl.pallas_call(
        _mixer_pre_kernel,
        grid=(s // tm,),
        in_specs=in_specs,
        out_specs=out_specs,
        out_shape=out_shape,
        compiler_params=_params("parallel"),
        name="mixer_pre",
    )(*ins)


def _flash_kernel(q_ref, k_ref, v_ref, o_ref, *, tk):
    tq = q_ref.shape[0]
    s = k_ref.shape[0]
    q = q_ref[...]

    def body(j, carry):
        m, acc = carry
        start = pl.multiple_of(j * tk, tk)
        kt = k_ref[pl.ds(start, tk), :]
        vt = v_ref[pl.ds(start, tk), :]
        sc = _dot_nt(q, kt)
        m_new = jnp.maximum(m, jnp.max(sc, axis=-1, keepdims=True))
        alpha = jnp.exp(m - m_new)
        p = jnp.exp(sc - m_new).astype(BF16)
        return m_new, acc * alpha + _dot(p, vt)

    m0 = jnp.full((tq, 1), -jnp.inf, F32)
    acc0 = jnp.zeros((tq, HEAD_PAD), F32)
    _, acc = lax.fori_loop(0, s // tk, body, (m0, acc0))
    o_ref[...] = (acc / acc[:, MLA_V:MLA_V + 1]).astype(BF16)


def _flash(q, k, v, tq, tk):
    s = q.shape[0]
    kv_spec = pl.BlockSpec((s, HEAD_PAD), lambda h, i: (0, h))
    return pl.pallas_call(
        functools.partial(_flash_kernel, tk=tk),
        grid=(MLA_HEADS, s // tq),
        in_specs=[pl.BlockSpec((tq, HEAD_PAD), lambda h, i: (i, h)), kv_spec, kv_spec],
        out_specs=pl.BlockSpec((tq, HEAD_PAD), lambda h, i: (i, h)),
        out_shape=jax.ShapeDtypeStruct((s, QK_PAD), BF16),
        compiler_params=_params("parallel", "arbitrary"),
        name="flash",
    )(q, k, v)


def _mixer_post_kernel(x_ref, ya_ref, yb_ref, yc_ref, gmix_ref, wgate_ref, bgate_ref,
                       wbr0_ref, wbr1_ref, wbr2_ref, wout_ref, o_ref):
    x = x_ref[...]
    h = _rms(x, gmix_ref[...]).astype(BF16)
    mix = None
    for n, (y_ref, w_ref) in enumerate(((ya_ref, wbr0_ref), (yb_ref, wbr1_ref), (yc_ref, wbr2_ref))):
        cols = slice(n * D_MODEL, (n + 1) * D_MODEL)
        gate = jax.nn.sigmoid(_dot(h, wgate_ref[:, cols]) + bgate_ref[n:n + 1, :])
        term = gate * _dot(y_ref[...], w_ref[...])
        mix = term if mix is None else mix + term
    o_ref[...] = x + _dot(mix.astype(BF16), wout_ref[...])


def _mixer_post(x, ya, yb, yc, lw, tm):
    s = x.shape[0]
    consts = [lw["g_mix"], lw["w_gate"], lw["b_gate"], lw["w_br0"], lw["w_br1"], lw["w_br2"],
              lw["w_out"]]
    in_specs = [_row_spec(tm, D_MODEL), _row_spec(tm, QK_PAD), _row_spec(tm, BRANCH_WIDTH),
                _row_spec(tm, BRANCH_WIDTH)] + [_const_spec(a.shape) for a in consts]
    return pl.pallas_call(
        _mixer_post_kernel,
        grid=(s // tm,),
        in_specs=in_specs,
        out_specs=_row_spec(tm, D_MODEL),
        out_shape=jax.ShapeDtypeStruct((s, D_MODEL), F32),
        compiler_params=_params("parallel"),
        name="mixer_post",
    )(x, ya, yb, yc, *consts)


def _dense_ffn_kernel(x_ref, g_ref, w1_ref, w3_ref, w2_ref, o_ref):
    x = x_ref[...]
    h = _rms(x, g_ref[...]).astype(BF16)
    t = jax.nn.silu(_dot(h, w1_ref[...])) * _dot(h, w3_ref[...])
    o_ref[...] = x + _dot(t.astype(BF16), w2_ref[...])


def _dense_ffn(x, g, w1, w3, w2, tm):
    s = x.shape[0]
    consts = [g, w1, w3, w2]
    return pl.pallas_call(
        _dense_ffn_kernel,
        grid=(s // tm,),
        in_specs=[_row_spec(tm, D_MODEL)] + [_const_spec(a.shape) for a in consts],
        out_specs=_row_spec(tm, D_MODEL),
        out_shape=jax.ShapeDtypeStruct((s, D_MODEL), F32),
        compiler_params=_params("parallel"),
        name="dense_ffn",
    )(x, *consts)


def _moe_kernel(x_ref, g_ref, wr_ref, w1_ref, w3_ref, w2_ref, o_ref, h_scr, comb_scr):
    e = pl.program_id(1)
    c = pl.program_id(2)

    @pl.when((e == 0) & (c == 0))
    def _route():
        x = x_ref[...]
        h = _rms(x, g_ref[...]).astype(BF16)
        h_scr[...] = h
        o_ref[...] = x
        logits = _dot(h, wr_ref[...])
        lane = lax.broadcasted_iota(jnp.int32, logits.shape, 1)
        logits = jnp.where(lane < N_EXPERTS, logits, -jnp.inf)
        m1 = jnp.max(logits, axis=-1, keepdims=True)
        i1 = jnp.min(jnp.where(logits == m1, lane, LANE), axis=-1, keepdims=True)
        rest = jnp.where(lane == i1, -jnp.inf, logits)
        m2 = jnp.max(rest, axis=-1, keepdims=True)
        i2 = jnp.min(jnp.where(rest == m2, lane, LANE), axis=-1, keepdims=True)
        e2 = jnp.exp(m2 - m1)
        den = 1.0 + e2
        comb_scr[...] = jnp.where(lane == i1, 1.0 / den, 0.0) + jnp.where(lane == i2, e2 / den, 0.0)

    lane = lax.broadcasted_iota(jnp.int32, comb_scr.shape, 1)
    wcol = jnp.sum(jnp.where(lane == e, comb_scr[...], 0.0), axis=-1, keepdims=True)
    h = h_scr[...]
    t = jax.nn.silu(_dot(h, w1_ref[...])) * _dot(h, w3_ref[...])
    o_ref[...] += wcol * _dot(t.astype(BF16), w2_ref[...])


def _moe(x, g, w_router, w1, w3, w2, tm, fc):
    s = x.shape[0]
    d_ff = w1.shape[-1]
    return pl.pallas_call(
        _moe_kernel,
        grid=(s // tm, N_EXPERTS, d_ff // fc),
        in_specs=[
            pl.BlockSpec((tm, D_MODEL), lambda i, e, c: (i, 0)),
            pl.BlockSpec((1, D_MODEL), lambda i, e, c: (0, 0)),
            pl.BlockSpec((D_MODEL, LANE), lambda i, e, c: (0, 0)),
            pl.BlockSpec((None, D_MODEL, fc), lambda i, e, c: (e, 0, c)),
            pl.BlockSpec((None, D_MODEL, fc), lambda i, e, c: (e, 0, c)),
            pl.BlockSpec((None, fc, D_MODEL), lambda i, e, c: (e, c, 0)),
        ],
        out_specs=pl.BlockSpec((tm, D_MODEL), lambda i, e, c: (i, 0)),
        out_shape=jax.ShapeDtypeStruct((s, D_MODEL), F32),
        scratch_shapes=[pltpu.VMEM((tm, D_MODEL), BF16), pltpu.VMEM((tm, LANE), F32)],
        compiler_params=_params("parallel", "arbitrary", "arbitrary"),
        name="moe",
    )(x, g, w_router, w1, w3, w2)


def _final_norm_kernel(x_ref, g_ref, o_ref):
    o_ref[...] = _rms(x_ref[...], g_ref[...])


def _final_norm(x, g, tm):
    s = x.shape[0]
    return pl.pallas_call(
        _final_norm_kernel,
        grid=(s // tm,),
        in_specs=[_row_spec(tm, D_MODEL), _const_spec(g.shape)],
        out_specs=_row_spec(tm, D_MODEL),
        out_shape=jax.ShapeDtypeStruct((s, D_MODEL), F32),
        compiler_params=_params("parallel"),
        name="final_norm",
    )(x, g)


def _pad_heads(w, width):
    r = w.shape[0]
    w = w.reshape(r, MLA_HEADS, width)
    return jnp.pad(w, ((0, 0), (0, 0), (0, HEAD_PAD - width))).reshape(r, QK_PAD)


def _rotate_half_cols(w_rope):
    half = MLA_ROPE // 2
    return jnp.concatenate([-w_rope[..., half:], w_rope[..., :half]], axis=-1)


def _layer_weights(l, g_mix, w_in, g_q, g_kv, w_uq, w_uk, w_uv, g_sg, w_s, b_s, w_br, b_gate,
                   w_out):
    o_dkv = Q_RANK
    o_kr = o_dkv + KV_RANK
    o_sg = o_kr + MLA_ROPE
    o_xq = o_sg + 2 * SG_WIDTH
    o_gate = o_xq + X_HEADS * X_HEAD_DIM
    wi = w_in[l]
    w_kr = wi[:, o_kr:o_sg]
    rope_pad = ((0, 0), (MLA_NOPE, HEAD_PAD - MLA_QK))
    w_kr2 = jnp.concatenate([jnp.pad(w_kr, rope_pad), jnp.pad(_rotate_half_cols(w_kr), rope_pad)],
                            axis=1)
    uq = w_uq[l].reshape(Q_RANK, MLA_HEADS, MLA_QK)
    uq_rot = jnp.concatenate([jnp.zeros_like(uq[..., :MLA_NOPE]),
                              _rotate_half_cols(uq[..., MLA_NOPE:])], axis=-1)
    w_br0 = jnp.pad(w_br[l, 0].reshape(MLA_HEADS, MLA_V, D_MODEL),
                    ((0, 0), (0, HEAD_PAD - MLA_V), (0, 0))).reshape(QK_PAD, D_MODEL)
    return {
        "g_mix": g_mix[l][None, :],
        "w_dq": wi[:, :o_dkv].astype(BF16),
        "w_dkv": wi[:, o_dkv:o_kr].astype(BF16),
        "w_kr": w_kr2.astype(BF16),
        "w_sg": wi[:, o_sg:o_xq].astype(BF16),
        "w_xq": wi[:, o_xq:o_gate].astype(BF16),
        "w_gate": wi[:, o_gate:].astype(BF16),
        "g_q": g_q[l][None, :],
        "g_kv": g_kv[l][None, :],
        "w_uq": _pad_heads(w_uq[l], MLA_QK).astype(BF16),
        "w_uqr": _pad_heads(uq_rot.reshape(Q_RANK, MLA_HEADS * MLA_QK), MLA_QK).astype(BF16),
        "w_uk": _pad_heads(w_uk[l], MLA_NOPE).astype(BF16),
        "w_uv": _pad_heads(w_uv[l], MLA_V).astype(BF16),
        "g_sg": g_sg[l][None, :],
        "w_s": w_s[l].astype(BF16),
        "b_st": b_s[l].T,
        "b_gate": b_gate[l],
        "w_br0": w_br0.astype(BF16),
        "w_br1": w_br[l, 1].astype(BF16),
        "w_br2": w_br[l, 2].astype(BF16),
        "w_out": w_out[l].astype(BF16),
    }


def _rope_constants():
    half = MLA_ROPE // 2
    inv_freq = 1.0 / (ROPE_THETA ** (jnp.arange(0, MLA_ROPE, 2, dtype=F32) / MLA_ROPE))
    lane_freq = jnp.zeros((HEAD_PAD,), F32)
    lane_freq = lane_freq.at[MLA_NOPE:MLA_NOPE + half].set(inv_freq)
    lane_freq = lane_freq.at[MLA_NOPE + half:MLA_QK].set(inv_freq)
    v_one = jnp.zeros((MLA_HEADS, HEAD_PAD), F32).at[:, MLA_V].set(1.0).reshape(1, QK_PAD)
    return lane_freq[None, :], v_one


def _tiles(s):
    tm = min(512, s)
    tq = min(512, s)
    tk = min(512, s)
    tmoe = min(1024, s)
    return tm, tq, tk, tmoe


def kernel(x, mem, positions, g_mix, w_in, g_q, g_kv, w_uq, w_uk, w_uv, g_sg, w_s, b_s, g_mem, w_mkv, w_br, b_gate, w_out, g_ffn, dense_w1, dense_w3, dense_w2, w_router, moe_w1, moe_w3, moe_w2, g_final):
    b, s, d = x.shape
    assert b == 1 and d == D_MODEL and s % SG_CHUNK == 0
    depth = g_mix.shape[0]
    tm, tq, tk, tmoe = _tiles(s)
    xs = x[0]
    pos = positions[0].astype(F32)[:, None]
    inv_freq, v_one = _rope_constants()
    for l in range(depth):
        lw = _layer_weights(l, g_mix, w_in, g_q, g_kv, w_uq, w_uk, w_uv, g_sg, w_s, b_s, w_br,
                            b_gate, w_out)
        kv_mem = _mem_kv(mem[0], g_mem[None, :], w_mkv[l].astype(BF16))
        lw["k_mem_t"] = kv_mem[:, :BRANCH_WIDTH].T
        lw["v_mem"] = kv_mem[:, BRANCH_WIDTH:]
        lw["inv_freq"] = inv_freq
        lw["v_one"] = v_one
        q, k, v, yb, yc = _mixer_pre(xs, pos, lw, tm)
        ya = _flash(q, k, v, tq, tk)
        xs = _mixer_post(xs, ya, yb, yc, lw, tm)
        j = l // 2
        if l % 2 == 0:
            xs = _dense_ffn(xs, g_ffn[l][None, :], dense_w1[j].astype(BF16),
                            dense_w3[j].astype(BF16), dense_w2[j].astype(BF16), tm)
        else:
            wr = jnp.pad(w_router[j], ((0, 0), (0, LANE - N_EXPERTS))).astype(BF16)
            fc = 512 if moe_w1.shape[-1] % 512 == 0 else moe_w1.shape[-1]
            xs = _moe(xs, g_ffn[l][None, :], wr, moe_w1[j].astype(BF16), moe_w3[j].astype(BF16),
                      moe_w2[j].astype(BF16), tmoe, fc)
    return _final_norm(xs, g_final[None, :], tm)[None]
```

```python
import functools

import jax
import jax.numpy as jnp
from jax import lax
from jax.experimental import pallas as pl
from jax.experimental.pallas import tpu as pltpu

D_MODEL = 1024
N_BRANCH = 3
BRANCH_WIDTH = 512
MLA_HEADS = 8
MLA_NOPE = 64
MLA_ROPE = 32
MLA_QK = MLA_NOPE + MLA_ROPE
MLA_V = BRANCH_WIDTH // MLA_HEADS
Q_RANK = 384
KV_RANK = 256
ROPE_THETA = 10000.0
SG_CHUNK = 128
SG_GROUPS = 4
SG_WIDTH = BRANCH_WIDTH
X_HEADS = 4
X_HEAD_DIM = BRANCH_WIDTH // X_HEADS
N_EXPERTS = 8
EPS = 1e-6
LOG2_E = 1.4426950408889634

LANE = 128
HEAD_PAD = LANE
QK_PAD = MLA_HEADS * HEAD_PAD
V_ROWS = 80
VMEM_LIMIT = 56 * 1024 * 1024
FLASH_UNROLL = 4

F32 = jnp.float32
BF16 = jnp.bfloat16


def _dot(a, b):
    return jnp.dot(a, b, preferred_element_type=F32)


def _dot_nt(a, b):
    return lax.dot_general(a, b, (((1,), (1,)), ((), ())), preferred_element_type=F32)


def _rms(x, g):
    return x * lax.rsqrt(jnp.mean(x * x, axis=-1, keepdims=True) + EPS) * g


def _const_spec(shape):
    nd = len(shape)
    return pl.BlockSpec(shape, lambda *_: (0,) * nd, pipeline_mode=pl.Buffered(1))


def _row_spec(tm, width):
    return pl.BlockSpec((tm, width), lambda i: (i, 0))


def _params(*sem):
    return pltpu.CompilerParams(dimension_semantics=sem, vmem_limit_bytes=VMEM_LIMIT)


def _mem_kv_kernel(mem_ref, g_ref, w_ref, o_ref):
    mem_n = _rms(mem_ref[...], g_ref[...]).astype(BF16)
    o_ref[...] = _dot(mem_n, w_ref[...]).astype(BF16)


def _mem_kv(mem, g_mem, w_mkv):
    m = mem.shape[0]
    return pl.pallas_call(
        _mem_kv_kernel,
        out_shape=jax.ShapeDtypeStruct((m, 2 * BRANCH_WIDTH), BF16),
        name="mem_kv",
    )(mem, g_mem, w_mkv)


def _mixer_pre_kernel(x_ref, pos_ref, posr_ref, gmix_ref, wdq_ref, wdkv_ref, wkr_ref, wsg_ref,
                      wxq_ref, gq_ref, gkv_ref, wuqt_ref, wuqrt_ref, wuk_ref, wuvt_ref, invf_ref,
                      invfc_ref, vonec_ref, gsg_ref, ws_ref, bst_ref, kmt_ref, vm_ref,
                      qt_out, k_out, vt_out, yb_out, yc_out):
    tm = x_ref.shape[0]
    h = _rms(x_ref[...], gmix_ref[...]).astype(BF16)

    ang = pos_ref[...] * invf_ref[...]
    cs = jnp.cos(ang)
    sn = jnp.sin(ang)
    ang_t = invfc_ref[...] * posr_ref[...]
    cs_t = jnp.cos(ang_t)
    sn_t = jnp.sin(ang_t)

    cq = _rms(_dot(h, wdq_ref[...]), gq_ref[...]).astype(BF16)
    qa_t = _dot_nt(wuqt_ref[...], cq)
    qb_t = _dot_nt(wuqrt_ref[...], cq)
    ckv = _rms(_dot(h, wdkv_ref[...]), gkv_ref[...]).astype(BF16)
    kn = _dot(ckv, wuk_ref[...])
    kr = _dot(h, wkr_ref[...])
    k_rope = kr[:, :LANE] * cs + kr[:, LANE:] * sn
    scale = MLA_QK ** -0.5 * LOG2_E
    for hd in range(MLA_HEADS):
        sl = slice(hd * HEAD_PAD, (hd + 1) * HEAD_PAD)
        qt_out[sl, :] = ((qa_t[sl, :] * cs_t + qb_t[sl, :] * sn_t) * scale).astype(BF16)
        k_out[:, sl] = (kn[:, sl] + k_rope).astype(BF16)
    vt_out[...] = (_dot_nt(wuvt_ref[...], ckv) + vonec_ref[...]).astype(BF16)

    z = jax.nn.gelu(_dot(h, wsg_ref[...]))
    u = z[:, :SG_WIDTH]
    vn = _rms(z[:, SG_WIDTH:], gsg_ref[...]).astype(BF16)
    gw = SG_WIDTH // SG_GROUPS
    for n in range(tm // SG_CHUNK):
        rows = slice(n * SG_CHUNK, (n + 1) * SG_CHUNK)
        for g in range(SG_GROUPS):
            cols = slice(g * gw, (g + 1) * gw)
            sg = _dot(ws_ref[g], vn[rows, cols]) + bst_ref[:, g:g + 1]
            yb_out[rows, cols] = (u[rows, cols] * sg).astype(BF16)

    qx = _dot(h, wxq_ref[...]) * (X_HEAD_DIM ** -0.5)
    for hd in range(X_HEADS):
        sl = slice(hd * X_HEAD_DIM, (hd + 1) * X_HEAD_DIM)
        sc = _dot(qx[:, sl].astype(BF16), kmt_ref[sl, :])
        p = jnp.exp(sc - jnp.max(sc, axis=-1, keepdims=True))
        l = jnp.sum(p, axis=-1, keepdims=True)
        o = _dot(p.astype(BF16), vm_ref[:, sl])
        yc_out[:, sl] = (o / l).astype(BF16)


def _col_spec(height, tm):
    return pl.BlockSpec((height, tm), lambda i: (0, i))


def _mixer_pre(x, pos, lw, tm):
    s = x.shape[0]
    ins = [x, pos, pos.reshape(1, s), lw["g_mix"], lw["w_dq"], lw["w_dkv"], lw["w_kr"], lw["w_sg"],
           lw["w_xq"], lw["g_q"], lw["g_kv"], lw["w_uq_t"], lw["w_uqr_t"], lw["w_uk"],
           lw["w_uv_t"], lw["inv_freq"], lw["inv_freq"].T, lw["v_one"].T, lw["g_sg"], lw["w_s"],
           lw["b_st"], lw["k_mem_t"], lw["v_mem"]]
    in_specs = [_row_spec(tm, D_MODEL), _row_spec(tm, 1), _col_spec(1, tm)] + \
               [_const_spec(a.shape) for a in ins[3:]]
    v_rows = MLA_HEADS * V_ROWS
    out_shape = [jax.ShapeDtypeStruct((QK_PAD, s), BF16), jax.ShapeDtypeStruct((s, QK_PAD), BF16),
                 jax.ShapeDtypeStruct((v_rows, s), BF16)] + \
                [jax.ShapeDtypeStruct((s, BRANCH_WIDTH), BF16)] * 2
    out_specs = [_col_spec(QK_PAD, tm), _row_spec(tm, QK_PAD), _col_spec(v_rows, tm)] + \
                [_row_spec(tm, BRANCH_WIDTH)] * 2
    return pl.pallas_call(
        _mixer_pre_kernel,
        grid=(s // tm,),
        in_specs=in_specs,
        out_specs=out_specs,
        out_shape=out_shape,
        compiler_params=_params("parallel"),
        name="mixer_pre",
    )(*ins)


def _flash_kernel(qt_ref, k_ref, vt_ref, o_ref, s_scr, m_scr, acc_scr, *, tk):
    n = k_ref.shape[0] // tk
    qt = qt_ref[...]

    def scores(t, slot):
        start = pl.multiple_of(t * tk, tk)
        s_scr[slot] = _dot(k_ref[pl.ds(start, tk), :], qt)

    def update(t, slot):
        start = pl.multiple_of(t * tk, tk)
        sc = s_scr[slot]
        m = m_scr[...]
        m_new = jnp.maximum(m, jnp.max(sc, axis=0, keepdims=True))
        alpha = jnp.exp2(m - m_new)
        p = jnp.exp2(sc - m_new).astype(BF16)
        acc_scr[...] = acc_scr[...] * alpha + _dot(vt_ref[:, pl.ds(start, tk)], p)
        m_scr[...] = m_new

    m_scr[...] = jnp.full(m_scr.shape, -jnp.inf, F32)
    acc_scr[...] = jnp.zeros(acc_scr.shape, F32)
    scores(0, 0)
    loops = (n - 1) // FLASH_UNROLL

    def body(u, carry):
        base = u * FLASH_UNROLL
        for r in range(FLASH_UNROLL):
            scores(base + r + 1, (r + 1) % 2)
            update(base + r, r % 2)
        return carry

    lax.fori_loop(0, loops, body, 0)
    for t in range(loops * FLASH_UNROLL, n):
        if t + 1 < n:
            scores(t + 1, (t + 1) % 2)
        update(t, t % 2)
    acc = acc_scr[...]
    out_t = acc / acc[MLA_V:MLA_V + 1, :]
    pad = jnp.zeros((HEAD_PAD - V_ROWS, out_t.shape[1]), F32)
    o_ref[...] = jnp.concatenate([out_t, pad], axis=0).T.astype(BF16)


def _flash(qt, k, vt, tq, tk):
    s = k.shape[0]
    return pl.pallas_call(
        functools.partial(_flash_kernel, tk=tk),
        grid=(MLA_HEADS, s // tq),
        in_specs=[pl.BlockSpec((HEAD_PAD, tq), lambda h, i: (h, i)),
                  pl.BlockSpec((s, HEAD_PAD), lambda h, i: (0, h)),
                  pl.BlockSpec((V_ROWS, s), lambda h, i: (h, 0))],
        out_specs=pl.BlockSpec((tq, HEAD_PAD), lambda h, i: (i, h)),
        out_shape=jax.ShapeDtypeStruct((s, QK_PAD), BF16),
        scratch_shapes=[pltpu.VMEM((2, tk, tq), F32), pltpu.VMEM((1, tq), F32),
                        pltpu.VMEM((V_ROWS, tq), F32)],
        compiler_params=_params("parallel", "arbitrary"),
        name="flash",
    )(qt, k, vt)


def _mixer_post_kernel(x_ref, ya_ref, yb_ref, yc_ref, gmix_ref, wgate_ref, bgate_ref,
                       wbr0_ref, wbr1_ref, wbr2_ref, wout_ref, o_ref):
    x = x_ref[...]
    h = _rms(x, gmix_ref[...]).astype(BF16)
    mix = None
    for n, (y_ref, w_ref) in enumerate(((ya_ref, wbr0_ref), (yb_ref, wbr1_ref), (yc_ref, wbr2_ref))):
        cols = slice(n * D_MODEL, (n + 1) * D_MODEL)
        gate = jax.nn.sigmoid(_dot(h, wgate_ref[:, cols]) + bgate_ref[n:n + 1, :])
        term = gate * _dot(y_ref[...], w_ref[...])
        mix = term if mix is None else mix + term
    o_ref[...] = x + _dot(mix.astype(BF16), wout_ref[...])


def _mixer_post(x, ya, yb, yc, lw, tm):
    s = x.shape[0]
    consts = [lw["g_mix"], lw["w_gate"], lw["b_gate"], lw["w_br0"], lw["w_br1"], lw["w_br2"],
              lw["w_out"]]
    in_specs = [_row_spec(tm, D_MODEL), _row_spec(tm, QK_PAD), _row_spec(tm, BRANCH_WIDTH),
                _row_spec(tm, BRANCH_WIDTH)] + [_const_spec(a.shape) for a in consts]
    return pl.pallas_call(
        _mixer_post_kernel,
        grid=(s // tm,),
        in_specs=in_specs,
        out_specs=_row_spec(tm, D_MODEL),
        out_shape=jax.ShapeDtypeStruct((s, D_MODEL), F32),
        compiler_params=_params("parallel"),
        name="mixer_post",
    )(x, ya, yb, yc, *consts)


def _dense_ffn_kernel(x_ref, g_ref, w1_ref, w3_ref, w2_ref, o_ref):
    x = x_ref[...]
    h = _rms(x, g_ref[...]).astype(BF16)
    t = jax.nn.silu(_dot(h, w1_ref[...])) * _dot(h, w3_ref[...])
    o_ref[...] = x + _dot(t.astype(BF16), w2_ref[...])


def _dense_ffn(x, g, w1, w3, w2, tm):
    s = x.shape[0]
    consts = [g, w1, w3, w2]
    return pl.pallas_call(
        _dense_ffn_kernel,
        grid=(s // tm,),
        in_specs=[_row_spec(tm, D_MODEL)] + [_const_spec(a.shape) for a in consts],
        out_specs=_row_spec(tm, D_MODEL),
        out_shape=jax.ShapeDtypeStruct((s, D_MODEL), F32),
        compiler_params=_params("parallel"),
        name="dense_ffn",
    )(x, *consts)


def _moe_kernel(x_ref, g_ref, wr_ref, w1_ref, w3_ref, w2_ref, o_ref, h_scr, comb_scr):
    e = pl.program_id(1)
    c = pl.program_id(2)

    @pl.when((e == 0) & (c == 0))
    def _route():
        x = x_ref[...]
        h = _rms(x, g_ref[...]).astype(BF16)
        h_scr[...] = h
        o_ref[...] = x
        logits = _dot(h, wr_ref[...])
        lane = lax.broadcasted_iota(jnp.int32, logits.shape, 1)
        logits = jnp.where(lane < N_EXPERTS, logits, -jnp.inf)
        m1 = jnp.max(logits, axis=-1, keepdims=True)
        i1 = jnp.min(jnp.where(logits == m1, lane, LANE), axis=-1, keepdims=True)
        rest = jnp.where(lane == i1, -jnp.inf, logits)
        m2 = jnp.max(rest, axis=-1, keepdims=True)
        i2 = jnp.min(jnp.where(rest == m2, lane, LANE), axis=-1, keepdims=True)
        e2 = jnp.exp(m2 - m1)
        den = 1.0 + e2
        comb_scr[...] = jnp.where(lane == i1, 1.0 / den, 0.0) + jnp.where(lane == i2, e2 / den, 0.0)

    lane = lax.broadcasted_iota(jnp.int32, comb_scr.shape, 1)
    wcol = jnp.sum(jnp.where(lane == e, comb_scr[...], 0.0), axis=-1, keepdims=True)
    h = h_scr[...]
    t = jax.nn.silu(_dot(h, w1_ref[...])) * _dot(h, w3_ref[...])
    o_ref[...] += wcol * _dot(t.astype(BF16), w2_ref[...])


def _moe(x, g, w_router, w1, w3, w2, tm, fc):
    s = x.shape[0]
    d_ff = w1.shape[-1]
    return pl.pallas_call(
        _moe_kernel,
        grid=(s // tm, N_EXPERTS, d_ff // fc),
        in_specs=[
            pl.BlockSpec((tm, D_MODEL), lambda i, e, c: (i, 0)),
            pl.BlockSpec((1, D_MODEL), lambda i, e, c: (0, 0)),
            pl.BlockSpec((D_MODEL, LANE), lambda i, e, c: (0, 0)),
            pl.BlockSpec((None, D_MODEL, fc), lambda i, e, c: (e, 0, c)),
            pl.BlockSpec((None, D_MODEL, fc), lambda i, e, c: (e, 0, c)),
            pl.BlockSpec((None, fc, D_MODEL), lambda i, e, c: (e, c, 0)),
        ],
        out_specs=pl.BlockSpec((tm, D_MODEL), lambda i, e, c: (i, 0)),
        out_shape=jax.ShapeDtypeStruct((s, D_MODEL), F32),
        scratch_shapes=[pltpu.VMEM((tm, D_MODEL), BF16), pltpu.VMEM((tm, LANE), F32)],
        compiler_params=_params("parallel", "arbitrary", "arbitrary"),
        name="moe",
    )(x, g, w_router, w1, w3, w2)


def _final_norm_kernel(x_ref, g_ref, o_ref):
    o_ref[...] = _rms(x_ref[...], g_ref[...])


def _final_norm(x, g, tm):
    s = x.shape[0]
    return pl.pallas_call(
        _final_norm_kernel,
        grid=(s // tm,),
        in_specs=[_row_spec(tm, D_MODEL), _const_spec(g.shape)],
        out_specs=_row_spec(tm, D_MODEL),
        out_shape=jax.ShapeDtypeStruct((s, D_MODEL), F32),
        compiler_params=_params("parallel"),
        name="final_norm",
    )(x, g)


def _pad_heads(w, width, padded=HEAD_PAD):
    r = w.shape[0]
    w = w.reshape(r, MLA_HEADS, width)
    return jnp.pad(w, ((0, 0), (0, 0), (0, padded - width))).reshape(r, MLA_HEADS * padded)


def _rotate_half_cols(w_rope):
    half = MLA_ROPE // 2
    return jnp.concatenate([-w_rope[..., half:], w_rope[..., :half]], axis=-1)


def _layer_weights(l, g_mix, w_in, g_q, g_kv, w_uq, w_uk, w_uv, g_sg, w_s, b_s, w_br, b_gate,
                   w_out):
    o_dkv = Q_RANK
    o_kr = o_dkv + KV_RANK
    o_sg = o_kr + MLA_ROPE
    o_xq = o_sg + 2 * SG_WIDTH
    o_gate = o_xq + X_HEADS * X_HEAD_DIM
    wi = w_in[l]
    w_kr = wi[:, o_kr:o_sg]
    rope_pad = ((0, 0), (MLA_NOPE, HEAD_PAD - MLA_QK))
    w_kr2 = jnp.concatenate([jnp.pad(w_kr, rope_pad), jnp.pad(_rotate_half_cols(w_kr), rope_pad)],
                            axis=1)
    uq = w_uq[l].reshape(Q_RANK, MLA_HEADS, MLA_QK)
    uq_rot = jnp.concatenate([jnp.zeros_like(uq[..., :MLA_NOPE]),
                              _rotate_half_cols(uq[..., MLA_NOPE:])], axis=-1)
    w_br0 = jnp.pad(w_br[l, 0].reshape(MLA_HEADS, MLA_V, D_MODEL),
                    ((0, 0), (0, HEAD_PAD - MLA_V), (0, 0))).reshape(QK_PAD, D_MODEL)
    return {
        "g_mix": g_mix[l][None, :],
        "w_dq": wi[:, :o_dkv].astype(BF16),
        "w_dkv": wi[:, o_dkv:o_kr].astype(BF16),
        "w_kr": w_kr2.astype(BF16),
        "w_sg": wi[:, o_sg:o_xq].astype(BF16),
        "w_xq": wi[:, o_xq:o_gate].astype(BF16),
        "w_gate": wi[:, o_gate:].astype(BF16),
        "g_q": g_q[l][None, :],
        "g_kv": g_kv[l][None, :],
        "w_uq_t": _pad_heads(w_uq[l], MLA_QK).T.astype(BF16),
        "w_uqr_t": _pad_heads(uq_rot.reshape(Q_RANK, MLA_HEADS * MLA_QK), MLA_QK).T.astype(BF16),
        "w_uk": _pad_heads(w_uk[l], MLA_NOPE).astype(BF16),
        "w_uv_t": _pad_heads(w_uv[l], MLA_V, V_ROWS).T.astype(BF16),
        "g_sg": g_sg[l][None, :],
        "w_s": w_s[l].astype(BF16),
        "b_st": b_s[l].T,
        "b_gate": b_gate[l],
        "w_br0": w_br0.astype(BF16),
        "w_br1": w_br[l, 1].astype(BF16),
        "w_br2": w_br[l, 2].astype(BF16),
        "w_out": w_out[l].astype(BF16),
    }


def _rope_constants():
    half = MLA_ROPE // 2
    inv_freq = 1.0 / (ROPE_THETA ** (jnp.arange(0, MLA_ROPE, 2, dtype=F32) / MLA_ROPE))
    lane_freq = jnp.zeros((HEAD_PAD,), F32)
    lane_freq = lane_freq.at[MLA_NOPE:MLA_NOPE + half].set(inv_freq)
    lane_freq = lane_freq.at[MLA_NOPE + half:MLA_QK].set(inv_freq)
    v_one = jnp.zeros((MLA_HEADS, V_ROWS), F32).at[:, MLA_V].set(1.0).reshape(1, MLA_HEADS * V_ROWS)
    return lane_freq[None, :], v_one


def _tiles(s):
    tm = min(512, s)
    tq = min(512, s)
    tk = min(512, s)
    tmoe = min(1024, s)
    return tm, tq, tk, tmoe


def kernel(x, mem, positions, g_mix, w_in, g_q, g_kv, w_uq, w_uk, w_uv, g_sg, w_s, b_s, g_mem, w_mkv, w_br, b_gate, w_out, g_ffn, dense_w1, dense_w3, dense_w2, w_router, moe_w1, moe_w3, moe_w2, g_final):
    b, s, d = x.shape
    assert b == 1 and d == D_MODEL and s % SG_CHUNK == 0
    depth = g_mix.shape[0]
    tm, tq, tk, tmoe = _tiles(s)
    xs = x[0]
    pos = positions[0].astype(F32)[:, None]
    inv_freq, v_one = _rope_constants()
    for l in range(depth):
        lw = _layer_weights(l, g_mix, w_in, g_q, g_kv, w_uq, w_uk, w_uv, g_sg, w_s, b_s, w_br,
                            b_gate, w_out)
        kv_mem = _mem_kv(mem[0], g_mem[None, :], w_mkv[l].astype(BF16))
        lw["k_mem_t"] = kv_mem[:, :BRANCH_WIDTH].T
        lw["v_mem"] = kv_mem[:, BRANCH_WIDTH:]
        lw["inv_freq"] = inv_freq
        lw["v_one"] = v_one
        qt, k, vt, yb, yc = _mixer_pre(xs, pos, lw, tm)
        ya = _flash(qt, k, vt, tq, tk)
        xs = _mixer_post(xs, ya, yb, yc, lw, tm)
        j = l // 2
        if l % 2 == 0:
            xs = _dense_ffn(xs, g_ffn[l][None, :], dense_w1[j].astype(BF16),
                            dense_w3[j].astype(BF16), dense_w2[j].astype(BF16), tm)
        else:
            wr = jnp.pad(w_router[j], ((0, 0), (0, LANE - N_EXPERTS))).astype(BF16)
            fc = 512 if moe_w1.shape[-1] % 512 == 0 else moe_w1.shape[-1]
            xs = _moe(xs, g_ffn[l][None, :], wr, moe_w1[j].astype(BF16), moe_w3[j].astype(BF16),
                      moe_w2[j].astype(BF16), tmoe, fc)
    return _final_norm(xs, g_final[None, :], tm)[None]
```

```python
import functools

import jax
import jax.numpy as jnp
from jax import lax
from jax.experimental import pallas as pl
from jax.experimental.pallas import tpu as pltpu

D_MODEL = 1024
N_BRANCH = 3
BRANCH_WIDTH = 512
MLA_HEADS = 8
MLA_NOPE = 64
MLA_ROPE = 32
MLA_QK = MLA_NOPE + MLA_ROPE
MLA_V = BRANCH_WIDTH // MLA_HEADS
Q_RANK = 384
KV_RANK = 256
ROPE_THETA = 10000.0
SG_CHUNK = 128
SG_GROUPS = 4
SG_WIDTH = BRANCH_WIDTH
X_HEADS = 4
X_HEAD_DIM = BRANCH_WIDTH // X_HEADS
N_EXPERTS = 8
EPS = 1e-6
LOG2_E = 1.4426950408889634

LANE = 128
HEAD_PAD = LANE
QK_PAD = MLA_HEADS * HEAD_PAD
V_ROWS = 80
VMEM_LIMIT = 56 * 1024 * 1024
FLASH_UNROLL = 4

F32 = jnp.float32
BF16 = jnp.bfloat16


def _dot(a, b):
    return jnp.dot(a, b, preferred_element_type=F32)


def _dot_nt(a, b):
    return lax.dot_general(a, b, (((1,), (1,)), ((), ())), preferred_element_type=F32)


def _rms(x, g):
    return x * lax.rsqrt(jnp.mean(x * x, axis=-1, keepdims=True) + EPS) * g


def _const_spec(shape):
    nd = len(shape)
    return pl.BlockSpec(shape, lambda *_: (0,) * nd, pipeline_mode=pl.Buffered(1))


def _row_spec(tm, width):
    return pl.BlockSpec((tm, width), lambda i: (i, 0))


def _params(*sem):
    return pltpu.CompilerParams(dimension_semantics=sem, vmem_limit_bytes=VMEM_LIMIT)


def _mem_kv_kernel(mem_ref, g_ref, w_ref, o_ref):
    mem_n = _rms(mem_ref[...], g_ref[...]).astype(BF16)
    o_ref[...] = _dot(mem_n, w_ref[...]).astype(BF16)


def _mem_kv(mem, g_mem, w_mkv):
    m = mem.shape[0]
    return pl.pallas_call(
        _mem_kv_kernel,
        out_shape=jax.ShapeDtypeStruct((m, 2 * BRANCH_WIDTH), BF16),
        name="mem_kv",
    )(mem, g_mem, w_mkv)


def _mixer_pre_kernel(x_ref, pos_ref, posr_ref, gmix_ref, wdq_ref, wdkv_ref, wkr_ref, wsg_ref,
                      wxq_ref, gq_ref, gkv_ref, wuqt_ref, wuqrt_ref, wuk_ref, wuvt_ref, invf_ref,
                      invfc_ref, vonec_ref, gsg_ref, ws_ref, bst_ref, kmt_ref, vm_ref,
                      qt_out, k_out, vt_out, yb_out, yc_out):
    tm = x_ref.shape[0]
    h = _rms(x_ref[...], gmix_ref[...]).astype(BF16)

    ang = pos_ref[...] * invf_ref[...]
    cs = jnp.cos(ang)
    sn = jnp.sin(ang)
    ang_t = invfc_ref[...] * posr_ref[...]
    cs_t = jnp.cos(ang_t)
    sn_t = jnp.sin(ang_t)

    cq = _rms(_dot(h, wdq_ref[...]), gq_ref[...]).astype(BF16)
    qa_t = _dot_nt(wuqt_ref[...], cq)
    qb_t = _dot_nt(wuqrt_ref[...], cq)
    ckv = _rms(_dot(h, wdkv_ref[...]), gkv_ref[...]).astype(BF16)
    kn = _dot(ckv, wuk_ref[...])
    kr = _dot(h, wkr_ref[...])
    k_rope = kr[:, :LANE] * cs + kr[:, LANE:] * sn
    scale = MLA_QK ** -0.5 * LOG2_E
    for hd in range(MLA_HEADS):
        sl = slice(hd * HEAD_PAD, (hd + 1) * HEAD_PAD)
        qt_out[sl, :] = ((qa_t[sl, :] * cs_t + qb_t[sl, :] * sn_t) * scale).astype(BF16)
        k_out[:, sl] = (kn[:, sl] + k_rope).astype(BF16)
    vt_out[...] = (_dot_nt(wuvt_ref[...], ckv) + vonec_ref[...]).astype(BF16)

    z = jax.nn.gelu(_dot(h, wsg_ref[...]))
    u = z[:, :SG_WIDTH]
    vn = _rms(z[:, SG_WIDTH:], gsg_ref[...]).astype(BF16)
    gw = SG_WIDTH // SG_GROUPS
    for n in range(tm // SG_CHUNK):
        rows = slice(n * SG_CHUNK, (n + 1) * SG_CHUNK)
        for g in range(SG_GROUPS):
            cols = slice(g * gw, (g + 1) * gw)
            sg = _dot(ws_ref[g], vn[rows, cols]) + bst_ref[:, g:g + 1]
            yb_out[rows, cols] = (u[rows, cols] * sg).astype(BF16)

    qx = _dot(h, wxq_ref[...]) * (X_HEAD_DIM ** -0.5)
    for hd in range(X_HEADS):
        sl = slice(hd * X_HEAD_DIM, (hd + 1) * X_HEAD_DIM)
        sc = _dot(qx[:, sl].astype(BF16), kmt_ref[sl, :])
        p = jnp.exp(sc - jnp.max(sc, axis=-1, keepdims=True))
        l = jnp.sum(p, axis=-1, keepdims=True)
        o = _dot(p.astype(BF16), vm_ref[:, sl])
        yc_out[:, sl] = (o / l).astype(BF16)


def _col_spec(height, tm):
    return pl.BlockSpec((height, tm), lambda i: (0, i))


def _mixer_pre(x, pos, lw, tm):
    s = x.shape[0]
    ins = [x, pos, pos.reshape(1, s), lw["g_mix"], lw["w_dq"], lw["w_dkv"], lw["w_kr"], lw["w_sg"],
           lw["w_xq"], lw["g_q"], lw["g_kv"], lw["w_uq_t"], lw["w_uqr_t"], lw["w_uk"],
           lw["w_uv_t"], lw["inv_freq"], lw["inv_freq"].T, lw["v_one"].T, lw["g_sg"], lw["w_s"],
           lw["b_st"], lw["k_mem_t"], lw["v_mem"]]
    in_specs = [_row_spec(tm, D_MODEL), _row_spec(tm, 1), _col_spec(1, tm)] + \
               [_const_spec(a.shape) for a in ins[3:]]
    v_rows = MLA_HEADS * V_ROWS
    out_shape = [jax.ShapeDtypeStruct((QK_PAD, s), BF16), jax.ShapeDtypeStruct((s, QK_PAD), BF16),
                 jax.ShapeDtypeStruct((v_rows, s), BF16)] + \
                [jax.ShapeDtypeStruct((s, BRANCH_WIDTH), BF16)] * 2
    out_specs = [_col_spec(QK_PAD, tm), _row_spec(tm, QK_PAD), _col_spec(v_rows, tm)] + \
                [_row_spec(tm, BRANCH_WIDTH)] * 2
    return pl.pallas_call(
        _mixer_pre_kernel,
        grid=(s // tm,),
        in_specs=in_specs,
        out_specs=out_specs,
        out_shape=out_shape,
        compiler_params=_params("arbitrary"),
        name="mixer_pre",
    )(*ins)


def _flash_kernel(qt_ref, k_ref, vt_ref, o_ref, s_scr, m_scr, acc_scr, *, tk):
    n = k_ref.shape[0] // tk
    qt = qt_ref[...]

    def scores(t, slot):
        start = pl.multiple_of(t * tk, tk)
        s_scr[slot] = _dot(k_ref[pl.ds(start, tk), :], qt)

    def update(t, slot):
        start = pl.multiple_of(t * tk, tk)
        sc = s_scr[slot]
        m = m_scr[...]
        m_new = jnp.maximum(m, jnp.max(sc, axis=0, keepdims=True))
        alpha = jnp.exp2(m - m_new)
        p = jnp.exp2(sc - m_new).astype(BF16)
        acc_scr[...] = acc_scr[...] * alpha + _dot(vt_ref[:, pl.ds(start, tk)], p)
        m_scr[...] = m_new

    m_scr[...] = jnp.full(m_scr.shape, -jnp.inf, F32)
    acc_scr[...] = jnp.zeros(acc_scr.shape, F32)
    scores(0, 0)
    loops = (n - 1) // FLASH_UNROLL

    def body(u, carry):
        base = u * FLASH_UNROLL
        for r in range(FLASH_UNROLL):
            scores(base + r + 1, (r + 1) % 2)
            update(base + r, r % 2)
        return carry

    lax.fori_loop(0, loops, body, 0)
    for t in range(loops * FLASH_UNROLL, n):
        if t + 1 < n:
            scores(t + 1, (t + 1) % 2)
        update(t, t % 2)
    acc = acc_scr[...]
    out_t = acc / acc[MLA_V:MLA_V + 1, :]
    pad = jnp.zeros((HEAD_PAD - V_ROWS, out_t.shape[1]), F32)
    o_ref[...] = jnp.concatenate([out_t, pad], axis=0).T.astype(BF16)


def _flash(qt, k, vt, tq, tk):
    s = k.shape[0]
    return pl.pallas_call(
        functools.partial(_flash_kernel, tk=tk),
        grid=(MLA_HEADS, s // tq),
        in_specs=[pl.BlockSpec((HEAD_PAD, tq), lambda h, i: (h, i)),
                  pl.BlockSpec((s, HEAD_PAD), lambda h, i: (0, h)),
                  pl.BlockSpec((V_ROWS, s), lambda h, i: (h, 0))],
        out_specs=pl.BlockSpec((tq, HEAD_PAD), lambda h, i: (i, h)),
        out_shape=jax.ShapeDtypeStruct((s, QK_PAD), BF16),
        scratch_shapes=[pltpu.VMEM((2, tk, tq), F32), pltpu.VMEM((1, tq), F32),
                        pltpu.VMEM((V_ROWS, tq), F32)],
        compiler_params=_params("arbitrary", "arbitrary"),
        name="flash",
    )(qt, k, vt)


def _mixer_post_kernel(x_ref, ya_ref, yb_ref, yc_ref, gmix_ref, wgate_ref, bgate_ref,
                       wbr0_ref, wbr1_ref, wbr2_ref, wout_ref, o_ref):
    x = x_ref[...]
    h = _rms(x, gmix_ref[...]).astype(BF16)
    mix = None
    for n, (y_ref, w_ref) in enumerate(((ya_ref, wbr0_ref), (yb_ref, wbr1_ref), (yc_ref, wbr2_ref))):
        cols = slice(n * D_MODEL, (n + 1) * D_MODEL)
        gate = jax.nn.sigmoid(_dot(h, wgate_ref[:, cols]) + bgate_ref[n:n + 1, :])
        term = gate * _dot(y_ref[...], w_ref[...])
        mix = term if mix is None else mix + term
    o_ref[...] = x + _dot(mix.astype(BF16), wout_ref[...])


def _mixer_post(x, ya, yb, yc, lw, tm):
    s = x.shape[0]
    consts = [lw["g_mix"], lw["w_gate"], lw["b_gate"], lw["w_br0"], lw["w_br1"], lw["w_br2"],
              lw["w_out"]]
    in_specs = [_row_spec(tm, D_MODEL), _row_spec(tm, QK_PAD), _row_spec(tm, BRANCH_WIDTH),
                _row_spec(tm, BRANCH_WIDTH)] + [_const_spec(a.shape) for a in consts]
    return pl.pallas_call(
        _mixer_post_kernel,
        grid=(s // tm,),
        in_specs=in_specs,
        out_specs=_row_spec(tm, D_MODEL),
        out_shape=jax.ShapeDtypeStruct((s, D_MODEL), F32),
        compiler_params=_params("arbitrary"),
        name="mixer_post",
    )(x, ya, yb, yc, *consts)


def _dense_ffn_kernel(x_ref, g_ref, w1_ref, w3_ref, w2_ref, o_ref):
    x = x_ref[...]
    h = _rms(x, g_ref[...]).astype(BF16)
    t = jax.nn.silu(_dot(h, w1_ref[...])) * _dot(h, w3_ref[...])
    o_ref[...] = x + _dot(t.astype(BF16), w2_ref[...])


def _dense_ffn(x, g, w1, w3, w2, tm):
    s = x.shape[0]
    consts = [g, w1, w3, w2]
    return pl.pallas_call(
        _dense_ffn_kernel,
        grid=(s // tm,),
        in_specs=[_row_spec(tm, D_MODEL)] + [_const_spec(a.shape) for a in consts],
        out_specs=_row_spec(tm, D_MODEL),
        out_shape=jax.ShapeDtypeStruct((s, D_MODEL), F32),
        compiler_params=_params("arbitrary"),
        name="dense_ffn",
    )(x, *consts)


def _route_kernel(x_ref, g_ref, wrt_ref, tri_ref, h_out, e_out, rank_out, wcol_out, cnt_out,
                  carry_scr):
    tm = x_ref.shape[0]

    @pl.when(pl.program_id(0) == 0)
    def _init():
        carry_scr[...] = jnp.zeros(carry_scr.shape, F32)

    h = _rms(x_ref[...], g_ref[...])
    h_out[...] = h
    logits = _dot_nt(wrt_ref[...], h.astype(BF16))[:N_EXPERTS, :]
    row = lax.broadcasted_iota(jnp.int32, logits.shape, 0)
    m1 = jnp.max(logits, axis=0, keepdims=True)
    i1 = jnp.min(jnp.where(logits == m1, row, N_EXPERTS), axis=0, keepdims=True)
    rest = jnp.where(row == i1, -jnp.inf, logits)
    m2 = jnp.max(rest, axis=0, keepdims=True)
    i2 = jnp.min(jnp.where(rest == m2, row, N_EXPERTS), axis=0, keepdims=True)
    e2 = jnp.exp(m2 - m1)
    den = 1.0 + e2
    oh1 = (row == i1).astype(F32)
    oh2 = (row == i2).astype(F32)
    cum = _dot(jnp.concatenate([oh1, oh2], axis=0).astype(BF16), tri_ref[...])
    tot = cum[:N_EXPERTS] + cum[N_EXPERTS:] + carry_scr[:, 0:1]
    rank1 = jnp.sum(oh1 * tot, axis=0, keepdims=True) - 1.0
    rank2 = jnp.sum(oh2 * tot, axis=0, keepdims=True) - 1.0
    e_out[...] = jnp.concatenate([i1, i2], axis=0)
    rank_out[...] = jnp.concatenate([rank1, rank2], axis=0).astype(jnp.int32)
    carry_scr[...] = jnp.broadcast_to(tot[:, tm - 1:tm], carry_scr.shape)
    cnt_out[...] = carry_scr[...]
    w_rows = jnp.concatenate([1.0 / den, e2 / den, jnp.zeros((LANE - 2, tm), F32)], axis=0)
    wcol_out[...] = w_rows.T


def _route(x, g, w_router_t, tm):
    s = x.shape[0]
    tri = (jnp.arange(tm)[:, None] <= jnp.arange(tm)[None, :]).astype(BF16)
    pair = lambda dt: jax.ShapeDtypeStruct((2, s), dt)
    return pl.pallas_call(
        _route_kernel,
        grid=(s // tm,),
        in_specs=[_row_spec(tm, D_MODEL), _const_spec(g.shape), _const_spec(w_router_t.shape),
                  _const_spec(tri.shape)],
        out_specs=[_row_spec(tm, D_MODEL), _col_spec(2, tm), _col_spec(2, tm), _row_spec(tm, LANE),
                   pl.BlockSpec((N_EXPERTS, LANE), lambda i: (0, 0))],
        out_shape=[jax.ShapeDtypeStruct((s, D_MODEL), F32), pair(jnp.int32), pair(jnp.int32),
                   jax.ShapeDtypeStruct((s, LANE), F32),
                   jax.ShapeDtypeStruct((N_EXPERTS, LANE), F32)],
        scratch_shapes=[pltpu.VMEM((N_EXPERTS, LANE), F32)],
        compiler_params=_params("arbitrary"),
        name="moe_route",
    )(x, g, w_router_t, tri)


def _row_copy(src_ref, src_row, dst_ref, dst_row, sem):
    return pltpu.make_async_copy(src_ref.at[pl.ds(src_row, 1)], dst_ref.at[pl.ds(dst_row, 1)], sem)


def _dispatch_kernel(pos_ref, h_ref, xs_in_ref, xs_ref, sem):
    del xs_in_ref
    tm = h_ref.shape[0]

    def issue(t, carry):
        for k in range(2):
            _row_copy(h_ref, t, xs_ref, pos_ref[k, t], sem).start()
        return carry

    lax.fori_loop(0, tm, issue, 0, unroll=8)
    for k in range(2):
        pltpu.make_async_copy(h_ref, xs_ref.at[pl.ds(0, tm)], sem).wait()


def _dispatch(pos, h, n_slots, tm):
    s = h.shape[0]
    xs0 = jnp.zeros((n_slots, D_MODEL), F32)
    return pl.pallas_call(
        _dispatch_kernel,
        grid=(s // tm,),
        in_specs=[pl.BlockSpec((2, tm), lambda i: (0, i), memory_space=pltpu.SMEM),
                  _row_spec(tm, D_MODEL), pl.BlockSpec(memory_space=pl.ANY)],
        out_specs=pl.BlockSpec(memory_space=pl.ANY),
        out_shape=jax.ShapeDtypeStruct((n_slots, D_MODEL), F32),
        scratch_shapes=[pltpu.SemaphoreType.DMA(())],
        input_output_aliases={2: 0},
        compiler_params=pltpu.CompilerParams(dimension_semantics=("arbitrary",),
                                             vmem_limit_bytes=VMEM_LIMIT, has_side_effects=True),
        name="moe_dispatch",
    )(pos, h, xs0)


def _expert_kernel(te_ref, nu_ref, xs_ref, w1_ref, w3_ref, w2_ref, ys_ref, h_scr):
    del te_ref
    c = pl.program_id(1)

    @pl.when(c == 0)
    def _load():
        h_scr[...] = xs_ref[...].astype(BF16)
        ys_ref[...] = jnp.zeros(ys_ref.shape, F32)

    @pl.when(pl.program_id(0) < nu_ref[0])
    def _ffn():
        h = h_scr[...]
        t = jax.nn.silu(_dot(h, w1_ref[...])) * _dot(h, w3_ref[...])
        ys_ref[...] += _dot(t.astype(BF16), w2_ref[...])


def _experts(tile_expert, n_used, xs, w1, w3, w2, tm, fc):
    n_slots = xs.shape[0]
    n_chunks = w1.shape[-1] // fc

    def chunk(i, c, nu):
        return jnp.where(i < nu[0], c, n_chunks - 1)

    grid_spec = pltpu.PrefetchScalarGridSpec(
        num_scalar_prefetch=2,
        grid=(n_slots // tm, n_chunks),
        in_specs=[
            pl.BlockSpec((tm, D_MODEL), lambda i, c, te, nu: (i, 0)),
            pl.BlockSpec((None, D_MODEL, fc), lambda i, c, te, nu: (te[i], 0, chunk(i, c, nu))),
            pl.BlockSpec((None, D_MODEL, fc), lambda i, c, te, nu: (te[i], 0, chunk(i, c, nu))),
            pl.BlockSpec((None, fc, D_MODEL), lambda i, c, te, nu: (te[i], chunk(i, c, nu), 0)),
        ],
        out_specs=pl.BlockSpec((tm, D_MODEL), lambda i, c, te, nu: (i, 0)),
        scratch_shapes=[pltpu.VMEM((tm, D_MODEL), BF16)],
    )
    return pl.pallas_call(
        _expert_kernel,
        grid_spec=grid_spec,
        out_shape=jax.ShapeDtypeStruct((n_slots, D_MODEL), F32),
        compiler_params=_params("arbitrary", "arbitrary"),
        name="moe_experts",
    )(tile_expert, n_used, xs, w1, w3, w2)


def _combine_kernel(pos_ref, x_ref, wcol_ref, ys_ref, o_ref, buf, sem):
    tm = x_ref.shape[0]

    def issue(t, carry):
        for k in range(2):
            _row_copy(ys_ref, pos_ref[k, t], buf.at[k], t, sem).start()
        return carry

    lax.fori_loop(0, tm, issue, 0, unroll=8)
    for k in range(2):
        pltpu.make_async_copy(ys_ref.at[pl.ds(0, tm)], buf.at[k], sem).wait()
    w = wcol_ref[...]
    o_ref[...] = x_ref[...] + w[:, 0:1] * buf[0] + w[:, 1:2] * buf[1]


def _combine(pos, x, wcol, ys, tm):
    s = x.shape[0]
    return pl.pallas_call(
        _combine_kernel,
        grid=(s // tm,),
        in_specs=[pl.BlockSpec((2, tm), lambda i: (0, i), memory_space=pltpu.SMEM),
                  _row_spec(tm, D_MODEL), _row_spec(tm, LANE), pl.BlockSpec(memory_space=pl.ANY)],
        out_specs=_row_spec(tm, D_MODEL),
        out_shape=jax.ShapeDtypeStruct((s, D_MODEL), F32),
        scratch_shapes=[pltpu.VMEM((2, tm, D_MODEL), F32), pltpu.SemaphoreType.DMA(())],
        compiler_params=_params("arbitrary"),
        name="moe_combine",
    )(pos, x, wcol, ys)


def _moe(x, g, w_router, w1, w3, w2, tm, fc):
    s = x.shape[0]
    n_tiles = 2 * s // tm + N_EXPERTS
    wrt = jnp.pad(w_router.T, ((0, 16 - N_EXPERTS), (0, 0))).astype(BF16)
    h, e, rank, wcol, cnt = _route(x, g, wrt, tm)
    counts = cnt[:, 0].astype(jnp.int32)
    padded = (counts + tm - 1) // tm * tm
    ends = jnp.cumsum(padded)
    pos = (ends - padded)[e] + rank
    tile_start = jnp.arange(n_tiles, dtype=jnp.int32) * tm
    tile_expert = jnp.minimum(jnp.sum(tile_start[:, None] >= ends[None, :], axis=1),
                              N_EXPERTS - 1).astype(jnp.int32)
    n_used = (ends[-1:] // tm).astype(jnp.int32)
    xs = _dispatch(pos, h, n_tiles * tm, tm)
    ys = _experts(tile_expert, n_used, xs, w1, w3, w2, tm, fc)
    return _combine(pos, x, wcol, ys, tm)


def _final_norm_kernel(x_ref, g_ref, o_ref):
    o_ref[...] = _rms(x_ref[...], g_ref[...])


def _final_norm(x, g, tm):
    s = x.shape[0]
    return pl.pallas_call(
        _final_norm_kernel,
        grid=(s // tm,),
        in_specs=[_row_spec(tm, D_MODEL), _const_spec(g.shape)],
        out_specs=_row_spec(tm, D_MODEL),
        out_shape=jax.ShapeDtypeStruct((s, D_MODEL), F32),
        compiler_params=_params("arbitrary"),
        name="final_norm",
    )(x, g)


def _pad_heads(w, width, padded=HEAD_PAD):
    r = w.shape[0]
    w = w.reshape(r, MLA_HEADS, width)
    return jnp.pad(w, ((0, 0), (0, 0), (0, padded - width))).reshape(r, MLA_HEADS * padded)


def _rotate_half_cols(w_rope):
    half = MLA_ROPE // 2
    return jnp.concatenate([-w_rope[..., half:], w_rope[..., :half]], axis=-1)


def _layer_weights(l, g_mix, w_in, g_q, g_kv, w_uq, w_uk, w_uv, g_sg, w_s, b_s, w_br, b_gate,
                   w_out):
    o_dkv = Q_RANK
    o_kr = o_dkv + KV_RANK
    o_sg = o_kr + MLA_ROPE
    o_xq = o_sg + 2 * SG_WIDTH
    o_gate = o_xq + X_HEADS * X_HEAD_DIM
    wi = w_in[l]
    w_kr = wi[:, o_kr:o_sg]
    rope_pad = ((0, 0), (MLA_NOPE, HEAD_PAD - MLA_QK))
    w_kr2 = jnp.concatenate([jnp.pad(w_kr, rope_pad), jnp.pad(_rotate_half_cols(w_kr), rope_pad)],
                            axis=1)
    uq = w_uq[l].reshape(Q_RANK, MLA_HEADS, MLA_QK)
    uq_rot = jnp.concatenate([jnp.zeros_like(uq[..., :MLA_NOPE]),
                              _rotate_half_cols(uq[..., MLA_NOPE:])], axis=-1)
    w_br0 = jnp.pad(w_br[l, 0].reshape(MLA_HEADS, MLA_V, D_MODEL),
                    ((0, 0), (0, HEAD_PAD - MLA_V), (0, 0))).reshape(QK_PAD, D_MODEL)
    return {
        "g_mix": g_mix[l][None, :],
        "w_dq": wi[:, :o_dkv].astype(BF16),
        "w_dkv": wi[:, o_dkv:o_kr].astype(BF16),
        "w_kr": w_kr2.astype(BF16),
        "w_sg": wi[:, o_sg:o_xq].astype(BF16),
        "w_xq": wi[:, o_xq:o_gate].astype(BF16),
        "w_gate": wi[:, o_gate:].astype(BF16),
        "g_q": g_q[l][None, :],
        "g_kv": g_kv[l][None, :],
        "w_uq_t": _pad_heads(w_uq[l], MLA_QK).T.astype(BF16),
        "w_uqr_t": _pad_heads(uq_rot.reshape(Q_RANK, MLA_HEADS * MLA_QK), MLA_QK).T.astype(BF16),
        "w_uk": _pad_heads(w_uk[l], MLA_NOPE).astype(BF16),
        "w_uv_t": _pad_heads(w_uv[l], MLA_V, V_ROWS).T.astype(BF16),
        "g_sg": g_sg[l][None, :],
        "w_s": w_s[l].astype(BF16),
        "b_st": b_s[l].T,
        "b_gate": b_gate[l],
        "w_br0": w_br0.astype(BF16),
        "w_br1": w_br[l, 1].astype(BF16),
        "w_br2": w_br[l, 2].astype(BF16),
        "w_out": w_out[l].astype(BF16),
    }


def _rope_constants():
    half = MLA_ROPE // 2
    inv_freq = 1.0 / (ROPE_THETA ** (jnp.arange(0, MLA_ROPE, 2, dtype=F32) / MLA_ROPE))
    lane_freq = jnp.zeros((HEAD_PAD,), F32)
    lane_freq = lane_freq.at[MLA_NOPE:MLA_NOPE + half].set(inv_freq)
    lane_freq = lane_freq.at[MLA_NOPE + half:MLA_QK].set(inv_freq)
    v_one = jnp.zeros((MLA_HEADS, V_ROWS), F32).at[:, MLA_V].set(1.0).reshape(1, MLA_HEADS * V_ROWS)
    return lane_freq[None, :], v_one


def _tiles(s):
    tm = min(512, s)
    tq = min(512, s)
    tk = min(512, s)
    tmoe = min(512, s)
    return tm, tq, tk, tmoe


def kernel(x, mem, positions, g_mix, w_in, g_q, g_kv, w_uq, w_uk, w_uv, g_sg, w_s, b_s, g_mem, w_mkv, w_br, b_gate, w_out, g_ffn, dense_w1, dense_w3, dense_w2, w_router, moe_w1, moe_w3, moe_w2, g_final):
    b, s, d = x.shape
    assert b == 1 and d == D_MODEL and s % SG_CHUNK == 0
    depth = g_mix.shape[0]
    tm, tq, tk, tmoe = _tiles(s)
    xs = x[0]
    pos = positions[0].astype(F32)[:, None]
    inv_freq, v_one = _rope_constants()
    for l in range(depth):
        lw = _layer_weights(l, g_mix, w_in, g_q, g_kv, w_uq, w_uk, w_uv, g_sg, w_s, b_s, w_br,
                            b_gate, w_out)
        kv_mem = _mem_kv(mem[0], g_mem[None, :], w_mkv[l].astype(BF16))
        lw["k_mem_t"] = kv_mem[:, :BRANCH_WIDTH].T
        lw["v_mem"] = kv_mem[:, BRANCH_WIDTH:]
        lw["inv_freq"] = inv_freq
        lw["v_one"] = v_one
        qt, k, vt, yb, yc = _mixer_pre(xs, pos, lw, tm)
        ya = _flash(qt, k, vt, tq, tk)
        xs = _mixer_post(xs, ya, yb, yc, lw, tm)
        j = l // 2
        if l % 2 == 0:
            xs = _dense_ffn(xs, g_ffn[l][None, :], dense_w1[j].astype(BF16),
                            dense_w3[j].astype(BF16), dense_w2[j].astype(BF16), tm)
        else:
            fc = 512 if moe_w1.shape[-1] % 512 == 0 else moe_w1.shape[-1]
            xs = _moe(xs, g_ffn[l][None, :], w_router[j], moe_w1[j].astype(BF16),
                      moe_w3[j].astype(BF16), moe_w2[j].astype(BF16), tmoe, fc)
    return _final_norm(xs, g_final[None, :], tm)[None]
```

```python
import functools

import jax
import jax.numpy as jnp
from jax import lax
from jax.experimental import pallas as pl
from jax.experimental.pallas import tpu as pltpu

D_MODEL = 1024
N_BRANCH = 3
BRANCH_WIDTH = 512
MLA_HEADS = 8
MLA_NOPE = 64
MLA_ROPE = 32
MLA_QK = MLA_NOPE + MLA_ROPE
MLA_V = BRANCH_WIDTH // MLA_HEADS
Q_RANK = 384
KV_RANK = 256
ROPE_THETA = 10000.0
SG_CHUNK = 128
SG_GROUPS = 4
SG_WIDTH = BRANCH_WIDTH
X_HEADS = 4
X_HEAD_DIM = BRANCH_WIDTH // X_HEADS
N_EXPERTS = 8
EPS = 1e-6
LOG2_E = 1.4426950408889634

LANE = 128
HEAD_PAD = LANE
QK_PAD = MLA_HEADS * HEAD_PAD
V_ROWS = 80
VMEM_LIMIT = 56 * 1024 * 1024
FLASH_UNROLL = 8

F32 = jnp.float32
BF16 = jnp.bfloat16


def _dot(a, b):
    return jnp.dot(a, b, preferred_element_type=F32)


def _dot_nt(a, b):
    return lax.dot_general(a, b, (((1,), (1,)), ((), ())), preferred_element_type=F32)


def _rms(x, g):
    return x * lax.rsqrt(jnp.mean(x * x, axis=-1, keepdims=True) + EPS) * g


def _const_spec(shape):
    nd = len(shape)
    return pl.BlockSpec(shape, lambda *_: (0,) * nd, pipeline_mode=pl.Buffered(1))


def _row_spec(tm, width):
    return pl.BlockSpec((tm, width), lambda i: (i, 0))


def _params(*sem):
    return pltpu.CompilerParams(dimension_semantics=sem, vmem_limit_bytes=VMEM_LIMIT)


def _mem_kv_kernel(mem_ref, g_ref, w_ref, o_ref):
    mem_n = _rms(mem_ref[...], g_ref[...]).astype(BF16)
    o_ref[...] = _dot(mem_n, w_ref[...]).astype(BF16)


def _mem_kv(mem, g_mem, w_mkv):
    m = mem.shape[0]
    return pl.pallas_call(
        _mem_kv_kernel,
        out_shape=jax.ShapeDtypeStruct((m, 2 * BRANCH_WIDTH), BF16),
        name="mem_kv",
    )(mem, g_mem, w_mkv)


def _mixer_pre_kernel(x_ref, pos_ref, posr_ref, gmix_ref, wdq_ref, wdkv_ref, wkr_ref, wsg_ref,
                      wxq_ref, gq_ref, gkv_ref, wuqt_ref, wuqrt_ref, wuk_ref, wuvt_ref, invf_ref,
                      invfc_ref, vonec_ref, gsg_ref, ws_ref, bst_ref, kmt_ref, vm_ref,
                      qt_out, k_out, vt_out, yb_out, yc_out):
    tm = x_ref.shape[0]
    h = _rms(x_ref[...], gmix_ref[...]).astype(BF16)

    ang = pos_ref[...] * invf_ref[...]
    cs = jnp.cos(ang)
    sn = jnp.sin(ang)
    ang_t = invfc_ref[...] * posr_ref[...]
    cs_t = jnp.cos(ang_t)
    sn_t = jnp.sin(ang_t)

    cq = _rms(_dot(h, wdq_ref[...]), gq_ref[...]).astype(BF16)
    qa_t = _dot_nt(wuqt_ref[...], cq)
    qb_t = _dot_nt(wuqrt_ref[...], cq)
    ckv = _rms(_dot(h, wdkv_ref[...]), gkv_ref[...]).astype(BF16)
    kn = _dot(ckv, wuk_ref[...])
    kr = _dot(h, wkr_ref[...])
    k_rope = kr[:, :LANE] * cs + kr[:, LANE:] * sn
    scale = MLA_QK ** -0.5 * LOG2_E
    for hd in range(MLA_HEADS):
        sl = slice(hd * HEAD_PAD, (hd + 1) * HEAD_PAD)
        qt_out[sl, :] = ((qa_t[sl, :] * cs_t + qb_t[sl, :] * sn_t) * scale).astype(BF16)
        k_out[:, sl] = (kn[:, sl] + k_rope).astype(BF16)
    vt_out[...] = (_dot_nt(wuvt_ref[...], ckv) + vonec_ref[...]).astype(BF16)

    z = jax.nn.gelu(_dot(h, wsg_ref[...]))
    u = z[:, :SG_WIDTH]
    vn = _rms(z[:, SG_WIDTH:], gsg_ref[...]).astype(BF16)
    gw = SG_WIDTH // SG_GROUPS
    for n in range(tm // SG_CHUNK):
        rows = slice(n * SG_CHUNK, (n + 1) * SG_CHUNK)
        for g in range(SG_GROUPS):
            cols = slice(g * gw, (g + 1) * gw)
            sg = _dot(ws_ref[g], vn[rows, cols]) + bst_ref[:, g:g + 1]
            yb_out[rows, cols] = (u[rows, cols] * sg).astype(BF16)

    qx = _dot(h, wxq_ref[...]) * (X_HEAD_DIM ** -0.5)
    for hd in range(X_HEADS):
        sl = slice(hd * X_HEAD_DIM, (hd + 1) * X_HEAD_DIM)
        sc = _dot(qx[:, sl].astype(BF16), kmt_ref[sl, :])
        p = jnp.exp(sc - jnp.max(sc, axis=-1, keepdims=True))
        l = jnp.sum(p, axis=-1, keepdims=True)
        o = _dot(p.astype(BF16), vm_ref[:, sl])
        yc_out[:, sl] = (o / l).astype(BF16)


def _col_spec(height, tm):
    return pl.BlockSpec((height, tm), lambda i: (0, i))


def _mixer_pre(x, pos, lw, tm):
    s = x.shape[0]
    ins = [x, pos, pos.reshape(1, s), lw["g_mix"], lw["w_dq"], lw["w_dkv"], lw["w_kr"], lw["w_sg"],
           lw["w_xq"], lw["g_q"], lw["g_kv"], lw["w_uq_t"], lw["w_uqr_t"], lw["w_uk"],
           lw["w_uv_t"], lw["inv_freq"], lw["inv_freq"].T, lw["v_one"].T, lw["g_sg"], lw["w_s"],
           lw["b_st"], lw["k_mem_t"], lw["v_mem"]]
    in_specs = [_row_spec(tm, D_MODEL), _row_spec(tm, 1), _col_spec(1, tm)] + \
               [_const_spec(a.shape) for a in ins[3:]]
    v_rows = MLA_HEADS * V_ROWS
    out_shape = [jax.ShapeDtypeStruct((QK_PAD, s), BF16), jax.ShapeDtypeStruct((s, QK_PAD), BF16),
                 jax.ShapeDtypeStruct((v_rows, s), BF16)] + \
                [jax.ShapeDtypeStruct((s, BRANCH_WIDTH), BF16)] * 2
    out_specs = [_col_spec(QK_PAD, tm), _row_spec(tm, QK_PAD), _col_spec(v_rows, tm)] + \
                [_row_spec(tm, BRANCH_WIDTH)] * 2
    return pl.pallas_call(
        _mixer_pre_kernel,
        grid=(s // tm,),
        in_specs=in_specs,
        out_specs=out_specs,
        out_shape=out_shape,
        compiler_params=_params("arbitrary"),
        name="mixer_pre",
    )(*ins)


def _flash_kernel(qt_ref, k_ref, vt_ref, o_ref, s_scr, m_scr, acc_scr, *, tk):
    n = k_ref.shape[0] // tk
    qt = qt_ref[...]

    def scores(t, slot):
        start = pl.multiple_of(t * tk, tk)
        s_scr[slot] = _dot(k_ref[pl.ds(start, tk), :], qt)

    def update(t, slot):
        start = pl.multiple_of(t * tk, tk)
        sc = s_scr[slot]
        m = m_scr[...]
        m_new = jnp.maximum(m, jnp.max(sc, axis=0, keepdims=True))
        alpha = jnp.exp2(m - m_new)
        p = jnp.exp2(sc - m_new).astype(BF16)
        acc_scr[...] = acc_scr[...] * alpha + _dot(vt_ref[:, pl.ds(start, tk)], p)
        m_scr[...] = m_new

    m_scr[...] = jnp.full(m_scr.shape, -jnp.inf, F32)
    acc_scr[...] = jnp.zeros(acc_scr.shape, F32)
    scores(0, 0)
    loops = (n - 1) // FLASH_UNROLL

    def body(u, carry):
        base = u * FLASH_UNROLL
        for r in range(FLASH_UNROLL):
            scores(base + r + 1, (r + 1) % 2)
            update(base + r, r % 2)
        return carry

    lax.fori_loop(0, loops, body, 0)
    for t in range(loops * FLASH_UNROLL, n):
        if t + 1 < n:
            scores(t + 1, (t + 1) % 2)
        update(t, t % 2)
    acc = acc_scr[...]
    out_t = acc / acc[MLA_V:MLA_V + 1, :]
    pad = jnp.zeros((HEAD_PAD - V_ROWS, out_t.shape[1]), F32)
    o_ref[...] = jnp.concatenate([out_t, pad], axis=0).T.astype(BF16)


def _flash(qt, k, vt, tq, tk):
    s = k.shape[0]
    return pl.pallas_call(
        functools.partial(_flash_kernel, tk=tk),
        grid=(MLA_HEADS, s // tq),
        in_specs=[pl.BlockSpec((HEAD_PAD, tq), lambda h, i: (h, i)),
                  pl.BlockSpec((s, HEAD_PAD), lambda h, i: (0, h)),
                  pl.BlockSpec((V_ROWS, s), lambda h, i: (h, 0))],
        out_specs=pl.BlockSpec((tq, HEAD_PAD), lambda h, i: (i, h)),
        out_shape=jax.ShapeDtypeStruct((s, QK_PAD), BF16),
        scratch_shapes=[pltpu.VMEM((2, tk, tq), F32), pltpu.VMEM((1, tq), F32),
                        pltpu.VMEM((V_ROWS, tq), F32)],
        compiler_params=_params("arbitrary", "arbitrary"),
        name="flash",
    )(qt, k, vt)


def _mixer_post_kernel(x_ref, ya_ref, yb_ref, yc_ref, gmix_ref, wgate_ref, bgate_ref,
                       wbr0_ref, wbr1_ref, wbr2_ref, wout_ref, o_ref):
    x = x_ref[...]
    h = _rms(x, gmix_ref[...]).astype(BF16)
    mix = None
    for n, (y_ref, w_ref) in enumerate(((ya_ref, wbr0_ref), (yb_ref, wbr1_ref), (yc_ref, wbr2_ref))):
        cols = slice(n * D_MODEL, (n + 1) * D_MODEL)
        gate = jax.nn.sigmoid(_dot(h, wgate_ref[:, cols]) + bgate_ref[n:n + 1, :])
        term = gate * _dot(y_ref[...], w_ref[...])
        mix = term if mix is None else mix + term
    o_ref[...] = x + _dot(mix.astype(BF16), wout_ref[...])


def _mixer_post(x, ya, yb, yc, lw, tm):
    s = x.shape[0]
    consts = [lw["g_mix"], lw["w_gate"], lw["b_gate"], lw["w_br0"], lw["w_br1"], lw["w_br2"],
              lw["w_out"]]
    in_specs = [_row_spec(tm, D_MODEL), _row_spec(tm, QK_PAD), _row_spec(tm, BRANCH_WIDTH),
                _row_spec(tm, BRANCH_WIDTH)] + [_const_spec(a.shape) for a in consts]
    return pl.pallas_call(
        _mixer_post_kernel,
        grid=(s // tm,),
        in_specs=in_specs,
        out_specs=_row_spec(tm, D_MODEL),
        out_shape=jax.ShapeDtypeStruct((s, D_MODEL), F32),
        compiler_params=_params("arbitrary"),
        name="mixer_post",
    )(x, ya, yb, yc, *consts)


def _dense_ffn_kernel(x_ref, g_ref, w1_ref, w3_ref, w2_ref, o_ref):
    x = x_ref[...]
    h = _rms(x, g_ref[...]).astype(BF16)
    t = jax.nn.silu(_dot(h, w1_ref[...])) * _dot(h, w3_ref[...])
    o_ref[...] = x + _dot(t.astype(BF16), w2_ref[...])


def _dense_ffn(x, g, w1, w3, w2, tm):
    s = x.shape[0]
    consts = [g, w1, w3, w2]
    return pl.pallas_call(
        _dense_ffn_kernel,
        grid=(s // tm,),
        in_specs=[_row_spec(tm, D_MODEL)] + [_const_spec(a.shape) for a in consts],
        out_specs=_row_spec(tm, D_MODEL),
        out_shape=jax.ShapeDtypeStruct((s, D_MODEL), F32),
        compiler_params=_params("arbitrary"),
        name="dense_ffn",
    )(x, *consts)


def _route_kernel(x_ref, g_ref, wrt_ref, tri_ref, h_out, e_out, rank_out, wcol_out, cnt_out,
                  carry_scr):
    tm = x_ref.shape[0]

    @pl.when(pl.program_id(0) == 0)
    def _init():
        carry_scr[...] = jnp.zeros(carry_scr.shape, F32)

    h = _rms(x_ref[...], g_ref[...])
    h_out[...] = h
    logits = _dot_nt(wrt_ref[...], h.astype(BF16))[:N_EXPERTS, :]
    row = lax.broadcasted_iota(jnp.int32, logits.shape, 0)
    m1 = jnp.max(logits, axis=0, keepdims=True)
    i1 = jnp.min(jnp.where(logits == m1, row, N_EXPERTS), axis=0, keepdims=True)
    rest = jnp.where(row == i1, -jnp.inf, logits)
    m2 = jnp.max(rest, axis=0, keepdims=True)
    i2 = jnp.min(jnp.where(rest == m2, row, N_EXPERTS), axis=0, keepdims=True)
    e2 = jnp.exp(m2 - m1)
    den = 1.0 + e2
    oh1 = (row == i1).astype(F32)
    oh2 = (row == i2).astype(F32)
    cum = _dot(jnp.concatenate([oh1, oh2], axis=0).astype(BF16), tri_ref[...])
    tot = cum[:N_EXPERTS] + cum[N_EXPERTS:] + carry_scr[:, 0:1]
    rank1 = jnp.sum(oh1 * tot, axis=0, keepdims=True) - 1.0
    rank2 = jnp.sum(oh2 * tot, axis=0, keepdims=True) - 1.0
    e_out[...] = jnp.concatenate([i1, i2], axis=0)
    rank_out[...] = jnp.concatenate([rank1, rank2], axis=0).astype(jnp.int32)
    carry_scr[...] = jnp.broadcast_to(tot[:, tm - 1:tm], carry_scr.shape)
    cnt_out[...] = carry_scr[...]
    w_rows = jnp.concatenate([1.0 / den, e2 / den, jnp.zeros((LANE - 2, tm), F32)], axis=0)
    wcol_out[...] = w_rows.T


def _route(x, g, w_router_t, tm):
    s = x.shape[0]
    tri = (jnp.arange(tm)[:, None] <= jnp.arange(tm)[None, :]).astype(BF16)
    pair = lambda dt: jax.ShapeDtypeStruct((2, s), dt)
    return pl.pallas_call(
        _route_kernel,
        grid=(s // tm,),
        in_specs=[_row_spec(tm, D_MODEL), _const_spec(g.shape), _const_spec(w_router_t.shape),
                  _const_spec(tri.shape)],
        out_specs=[_row_spec(tm, D_MODEL), _col_spec(2, tm), _col_spec(2, tm), _row_spec(tm, LANE),
                   pl.BlockSpec((N_EXPERTS, LANE), lambda i: (0, 0))],
        out_shape=[jax.ShapeDtypeStruct((s, D_MODEL), F32), pair(jnp.int32), pair(jnp.int32),
                   jax.ShapeDtypeStruct((s, LANE), F32),
                   jax.ShapeDtypeStruct((N_EXPERTS, LANE), F32)],
        scratch_shapes=[pltpu.VMEM((N_EXPERTS, LANE), F32)],
        compiler_params=_params("arbitrary"),
        name="moe_route",
    )(x, g, w_router_t, tri)


def _row_copy(src_ref, src_row, dst_ref, dst_row, sem):
    return pltpu.make_async_copy(src_ref.at[pl.ds(src_row, 1)], dst_ref.at[pl.ds(dst_row, 1)], sem)


def _dispatch_kernel(pos_ref, h_ref, xs_in_ref, xs_ref, sem):
    del xs_in_ref
    tm = h_ref.shape[0]

    def issue(t, carry):
        for k in range(2):
            _row_copy(h_ref, t, xs_ref, pos_ref[k, t], sem).start()
        return carry

    lax.fori_loop(0, tm, issue, 0, unroll=8)
    for k in range(2):
        pltpu.make_async_copy(h_ref, xs_ref.at[pl.ds(0, tm)], sem).wait()


def _dispatch(pos, h, n_slots, tm):
    s = h.shape[0]
    xs0 = jnp.zeros((n_slots, D_MODEL), F32)
    return pl.pallas_call(
        _dispatch_kernel,
        grid=(s // tm,),
        in_specs=[pl.BlockSpec((2, tm), lambda i: (0, i), memory_space=pltpu.SMEM),
                  _row_spec(tm, D_MODEL), pl.BlockSpec(memory_space=pl.ANY)],
        out_specs=pl.BlockSpec(memory_space=pl.ANY),
        out_shape=jax.ShapeDtypeStruct((n_slots, D_MODEL), F32),
        scratch_shapes=[pltpu.SemaphoreType.DMA(())],
        input_output_aliases={2: 0},
        compiler_params=pltpu.CompilerParams(dimension_semantics=("arbitrary",),
                                             vmem_limit_bytes=VMEM_LIMIT, has_side_effects=True),
        name="moe_dispatch",
    )(pos, h, xs0)


def _expert_kernel(te_ref, nu_ref, xs_ref, w1_ref, w3_ref, w2_ref, ys_ref, h_scr):
    del te_ref
    c = pl.program_id(1)

    @pl.when(c == 0)
    def _load():
        h_scr[...] = xs_ref[...].astype(BF16)
        ys_ref[...] = jnp.zeros(ys_ref.shape, F32)

    @pl.when(pl.program_id(0) < nu_ref[0])
    def _ffn():
        h = h_scr[...]
        t = jax.nn.silu(_dot(h, w1_ref[...])) * _dot(h, w3_ref[...])
        ys_ref[...] += _dot(t.astype(BF16), w2_ref[...])


def _experts(tile_expert, n_used, xs, w1, w3, w2, tm, fc):
    n_slots = xs.shape[0]
    n_chunks = w1.shape[-1] // fc

    def chunk(i, c, nu):
        return jnp.where(i < nu[0], c, n_chunks - 1)

    grid_spec = pltpu.PrefetchScalarGridSpec(
        num_scalar_prefetch=2,
        grid=(n_slots // tm, n_chunks),
        in_specs=[
            pl.BlockSpec((tm, D_MODEL), lambda i, c, te, nu: (i, 0)),
            pl.BlockSpec((None, D_MODEL, fc), lambda i, c, te, nu: (te[i], 0, chunk(i, c, nu))),
            pl.BlockSpec((None, D_MODEL, fc), lambda i, c, te, nu: (te[i], 0, chunk(i, c, nu))),
            pl.BlockSpec((None, fc, D_MODEL), lambda i, c, te, nu: (te[i], chunk(i, c, nu), 0)),
        ],
        out_specs=pl.BlockSpec((tm, D_MODEL), lambda i, c, te, nu: (i, 0)),
        scratch_shapes=[pltpu.VMEM((tm, D_MODEL), BF16)],
    )
    return pl.pallas_call(
        _expert_kernel,
        grid_spec=grid_spec,
        out_shape=jax.ShapeDtypeStruct((n_slots, D_MODEL), F32),
        compiler_params=_params("arbitrary", "arbitrary"),
        name="moe_experts",
    )(tile_expert, n_used, xs, w1, w3, w2)


def _combine_kernel(pos_ref, x_ref, wcol_ref, ys_ref, o_ref, buf, sem):
    tm = x_ref.shape[0]

    def issue(t, carry):
        for k in range(2):
            _row_copy(ys_ref, pos_ref[k, t], buf.at[k], t, sem).start()
        return carry

    lax.fori_loop(0, tm, issue, 0, unroll=8)
    for k in range(2):
        pltpu.make_async_copy(ys_ref.at[pl.ds(0, tm)], buf.at[k], sem).wait()
    w = wcol_ref[...]
    o_ref[...] = x_ref[...] + w[:, 0:1] * buf[0] + w[:, 1:2] * buf[1]


def _combine(pos, x, wcol, ys, tm):
    s = x.shape[0]
    return pl.pallas_call(
        _combine_kernel,
        grid=(s // tm,),
        in_specs=[pl.BlockSpec((2, tm), lambda i: (0, i), memory_space=pltpu.SMEM),
                  _row_spec(tm, D_MODEL), _row_spec(tm, LANE), pl.BlockSpec(memory_space=pl.ANY)],
        out_specs=_row_spec(tm, D_MODEL),
        out_shape=jax.ShapeDtypeStruct((s, D_MODEL), F32),
        scratch_shapes=[pltpu.VMEM((2, tm, D_MODEL), F32), pltpu.SemaphoreType.DMA(())],
        compiler_params=_params("arbitrary"),
        name="moe_combine",
    )(pos, x, wcol, ys)


def _moe(x, g, w_router, w1, w3, w2, tm, fc):
    s = x.shape[0]
    n_tiles = 2 * s // tm + N_EXPERTS
    wrt = jnp.pad(w_router.T, ((0, 16 - N_EXPERTS), (0, 0))).astype(BF16)
    h, e, rank, wcol, cnt = _route(x, g, wrt, tm)
    counts = cnt[:, 0].astype(jnp.int32)
    padded = (counts + tm - 1) // tm * tm
    ends = jnp.cumsum(padded)
    offs = ends - padded
    pos = rank
    for ex in range(N_EXPERTS):
        pos = pos + jnp.where(e == ex, offs[ex], 0)
    tile_start = jnp.arange(n_tiles, dtype=jnp.int32) * tm
    tile_expert = jnp.minimum(jnp.sum(tile_start[:, None] >= ends[None, :], axis=1),
                              N_EXPERTS - 1).astype(jnp.int32)
    n_used = (ends[-1:] // tm).astype(jnp.int32)
    xs = _dispatch(pos, h, n_tiles * tm, tm)
    ys = _experts(tile_expert, n_used, xs, w1, w3, w2, tm, fc)
    return _combine(pos, x, wcol, ys, tm)


def _final_norm_kernel(x_ref, g_ref, o_ref):
    o_ref[...] = _rms(x_ref[...], g_ref[...])


def _final_norm(x, g, tm):
    s = x.shape[0]
    return pl.pallas_call(
        _final_norm_kernel,
        grid=(s // tm,),
        in_specs=[_row_spec(tm, D_MODEL), _const_spec(g.shape)],
        out_specs=_row_spec(tm, D_MODEL),
        out_shape=jax.ShapeDtypeStruct((s, D_MODEL), F32),
        compiler_params=_params("arbitrary"),
        name="final_norm",
    )(x, g)


def _pad_heads(w, width, padded=HEAD_PAD):
    r = w.shape[0]
    w = w.reshape(r, MLA_HEADS, width)
    return jnp.pad(w, ((0, 0), (0, 0), (0, padded - width))).reshape(r, MLA_HEADS * padded)


def _rotate_half_cols(w_rope):
    half = MLA_ROPE // 2
    return jnp.concatenate([-w_rope[..., half:], w_rope[..., :half]], axis=-1)


def _layer_weights(l, g_mix, w_in, g_q, g_kv, w_uq, w_uk, w_uv, g_sg, w_s, b_s, w_br, b_gate,
                   w_out):
    o_dkv = Q_RANK
    o_kr = o_dkv + KV_RANK
    o_sg = o_kr + MLA_ROPE
    o_xq = o_sg + 2 * SG_WIDTH
    o_gate = o_xq + X_HEADS * X_HEAD_DIM
    wi = w_in[l]
    w_kr = wi[:, o_kr:o_sg]
    rope_pad = ((0, 0), (MLA_NOPE, HEAD_PAD - MLA_QK))
    w_kr2 = jnp.concatenate([jnp.pad(w_kr, rope_pad), jnp.pad(_rotate_half_cols(w_kr), rope_pad)],
                            axis=1)
    uq = w_uq[l].reshape(Q_RANK, MLA_HEADS, MLA_QK)
    uq_rot = jnp.concatenate([jnp.zeros_like(uq[..., :MLA_NOPE]),
                              _rotate_half_cols(uq[..., MLA_NOPE:])], axis=-1)
    w_br0 = jnp.pad(w_br[l, 0].reshape(MLA_HEADS, MLA_V, D_MODEL),
                    ((0, 0), (0, HEAD_PAD - MLA_V), (0, 0))).reshape(QK_PAD, D_MODEL)
    return {
        "g_mix": g_mix[l][None, :],
        "w_dq": wi[:, :o_dkv].astype(BF16),
        "w_dkv": wi[:, o_dkv:o_kr].astype(BF16),
        "w_kr": w_kr2.astype(BF16),
        "w_sg": wi[:, o_sg:o_xq].astype(BF16),
        "w_xq": wi[:, o_xq:o_gate].astype(BF16),
        "w_gate": wi[:, o_gate:].astype(BF16),
        "g_q": g_q[l][None, :],
        "g_kv": g_kv[l][None, :],
        "w_uq_t": _pad_heads(w_uq[l], MLA_QK).T.astype(BF16),
        "w_uqr_t": _pad_heads(uq_rot.reshape(Q_RANK, MLA_HEADS * MLA_QK), MLA_QK).T.astype(BF16),
        "w_uk": _pad_heads(w_uk[l], MLA_NOPE).astype(BF16),
        "w_uv_t": _pad_heads(w_uv[l], MLA_V, V_ROWS).T.astype(BF16),
        "g_sg": g_sg[l][None, :],
        "w_s": w_s[l].astype(BF16),
        "b_st": b_s[l].T,
        "b_gate": b_gate[l],
        "w_br0": w_br0.astype(BF16),
        "w_br1": w_br[l, 1].astype(BF16),
        "w_br2": w_br[l, 2].astype(BF16),
        "w_out": w_out[l].astype(BF16),
    }


def _rope_constants():
    half = MLA_ROPE // 2
    inv_freq = 1.0 / (ROPE_THETA ** (jnp.arange(0, MLA_ROPE, 2, dtype=F32) / MLA_ROPE))
    lane_freq = jnp.zeros((HEAD_PAD,), F32)
    lane_freq = lane_freq.at[MLA_NOPE:MLA_NOPE + half].set(inv_freq)
    lane_freq = lane_freq.at[MLA_NOPE + half:MLA_QK].set(inv_freq)
    v_one = jnp.zeros((MLA_HEADS, V_ROWS), F32).at[:, MLA_V].set(1.0).reshape(1, MLA_HEADS * V_ROWS)
    return lane_freq[None, :], v_one


def _tiles(s):
    tm = min(512, s)
    tq = min(512, s)
    tk = min(512, s)
    tmoe = min(512, s)
    return tm, tq, tk, tmoe


def kernel(x, mem, positions, g_mix, w_in, g_q, g_kv, w_uq, w_uk, w_uv, g_sg, w_s, b_s, g_mem, w_mkv, w_br, b_gate, w_out, g_ffn, dense_w1, dense_w3, dense_w2, w_router, moe_w1, moe_w3, moe_w2, g_final):
    b, s, d = x.shape
    assert b == 1 and d == D_MODEL and s % SG_CHUNK == 0
    depth = g_mix.shape[0]
    tm, tq, tk, tmoe = _tiles(s)
    xs = x[0]
    pos = positions[0].astype(F32)[:, None]
    inv_freq, v_one = _rope_constants()
    for l in range(depth):
        lw = _layer_weights(l, g_mix, w_in, g_q, g_kv, w_uq, w_uk, w_uv, g_sg, w_s, b_s, w_br,
                            b_gate, w_out)
        kv_mem = _mem_kv(mem[0], g_mem[None, :], w_mkv[l].astype(BF16))
        lw["k_mem_t"] = kv_mem[:, :BRANCH_WIDTH].T
        lw["v_mem"] = kv_mem[:, BRANCH_WIDTH:]
        lw["inv_freq"] = inv_freq
        lw["v_one"] = v_one
        qt, k, vt, yb, yc = _mixer_pre(xs, pos, lw, tm)
        ya = _flash(qt, k, vt, tq, tk)
        xs = _mixer_post(xs, ya, yb, yc, lw, tm)
        j = l // 2
        if l % 2 == 0:
            xs = _dense_ffn(xs, g_ffn[l][None, :], dense_w1[j].astype(BF16),
                            dense_w3[j].astype(BF16), dense_w2[j].astype(BF16), tm)
        else:
            fc = 512 if moe_w1.shape[-1] % 512 == 0 else moe_w1.shape[-1]
            xs = _moe(xs, g_ffn[l][None, :], w_router[j], moe_w1[j].astype(BF16),
                      moe_w3[j].astype(BF16), moe_w2[j].astype(BF16), tmoe, fc)
    return _final_norm(xs, g_final[None, :], tm)[None]
```

```python
import functools

import jax
import jax.numpy as jnp
from jax import lax
from jax.experimental import pallas as pl
from jax.experimental.pallas import tpu as pltpu

D_MODEL = 1024
N_BRANCH = 3
BRANCH_WIDTH = 512
MLA_HEADS = 8
MLA_NOPE = 64
MLA_ROPE = 32
MLA_QK = MLA_NOPE + MLA_ROPE
MLA_V = BRANCH_WIDTH // MLA_HEADS
Q_RANK = 384
KV_RANK = 256
ROPE_THETA = 10000.0
SG_CHUNK = 128
SG_GROUPS = 4
SG_WIDTH = BRANCH_WIDTH
X_HEADS = 4
X_HEAD_DIM = BRANCH_WIDTH // X_HEADS
N_EXPERTS = 8
EPS = 1e-6
LOG2_E = 1.4426950408889634

LANE = 128
HEAD_PAD = LANE
QK_PAD = MLA_HEADS * HEAD_PAD
V_ROWS = 80
VMEM_LIMIT = 56 * 1024 * 1024
FLASH_AHEAD = 1
FLASH_UNROLL = 8

F32 = jnp.float32
BF16 = jnp.bfloat16


def _dot(a, b):
    return jnp.dot(a, b, preferred_element_type=F32)


def _dot_nt(a, b):
    return lax.dot_general(a, b, (((1,), (1,)), ((), ())), preferred_element_type=F32)


def _rms(x, g):
    return x * lax.rsqrt(jnp.mean(x * x, axis=-1, keepdims=True) + EPS) * g


def _const_spec(shape):
    nd = len(shape)
    return pl.BlockSpec(shape, lambda *_: (0,) * nd, pipeline_mode=pl.Buffered(1))


def _row_spec(tm, width):
    return pl.BlockSpec((tm, width), lambda i: (i, 0))


def _params(*sem):
    return pltpu.CompilerParams(dimension_semantics=sem, vmem_limit_bytes=VMEM_LIMIT)


def _mem_kv_kernel(mem_ref, g_ref, w_ref, o_ref):
    mem_n = _rms(mem_ref[...], g_ref[...]).astype(BF16)
    o_ref[...] = _dot(mem_n, w_ref[...]).astype(BF16)


def _mem_kv(mem, g_mem, w_mkv):
    m = mem.shape[0]
    return pl.pallas_call(
        _mem_kv_kernel,
        out_shape=jax.ShapeDtypeStruct((m, 2 * BRANCH_WIDTH), BF16),
        name="mem_kv",
    )(mem, g_mem, w_mkv)


def _rope_tables_kernel(pos_ref, posr_ref, invf_ref, invfc_ref, cs_out, sn_out, cst_out, snt_out):
    ang = pos_ref[...] * invf_ref[...]
    cs_out[...] = jnp.cos(ang)
    sn_out[...] = jnp.sin(ang)
    ang_t = invfc_ref[...] * posr_ref[...]
    cst_out[...] = jnp.cos(ang_t)
    snt_out[...] = jnp.sin(ang_t)


def _rope_tables(pos, inv_freq, tm):
    s = pos.shape[0]
    row = jax.ShapeDtypeStruct((s, HEAD_PAD), F32)
    col = jax.ShapeDtypeStruct((HEAD_PAD, s), F32)
    return pl.pallas_call(
        _rope_tables_kernel,
        grid=(s // tm,),
        in_specs=[_row_spec(tm, 1), pl.BlockSpec((1, tm), lambda i: (0, i)),
                  _const_spec(inv_freq.shape), _const_spec(inv_freq.T.shape)],
        out_specs=[_row_spec(tm, HEAD_PAD)] * 2 + [pl.BlockSpec((HEAD_PAD, tm), lambda i: (0, i))] * 2,
        out_shape=[row, row, col, col],
        compiler_params=_params("arbitrary"),
        name="rope_tables",
    )(pos, pos.reshape(1, s), inv_freq, inv_freq.T)


def _mixer_pre_kernel(x_ref, cs_ref, sn_ref, cst_ref, snt_ref, gmix_ref, wdq_ref, wdkv_ref,
                      wkr_ref, wsg_ref, wxq_ref, gq_ref, gkv_ref, wuqt_ref, wuqrt_ref, wuk_ref,
                      wuvt_ref, vonec_ref, gsg_ref, ws_ref, bst_ref, kmt_ref, vm_ref,
                      qt_out, k_out, vt_out, yb_out, yc_out):
    tm = x_ref.shape[0]
    h = _rms(x_ref[...], gmix_ref[...]).astype(BF16)

    cq = _rms(_dot(h, wdq_ref[...]), gq_ref[...]).astype(BF16)
    qa_t = _dot_nt(wuqt_ref[...], cq)
    qb_t = _dot_nt(wuqrt_ref[...], cq)
    ckv = _rms(_dot(h, wdkv_ref[...]), gkv_ref[...]).astype(BF16)
    kn = _dot(ckv, wuk_ref[...])
    kr = _dot(h, wkr_ref[...])
    k_rope = kr[:, :LANE] * cs_ref[...] + kr[:, LANE:] * sn_ref[...]
    scale = MLA_QK ** -0.5 * LOG2_E
    cs_r = cst_ref[MLA_NOPE:MLA_QK, :]
    sn_r = snt_ref[MLA_NOPE:MLA_QK, :]
    for hd in range(MLA_HEADS):
        sl = slice(hd * HEAD_PAD, (hd + 1) * HEAD_PAD)
        qa = qa_t[sl, :]
        rope = qa[MLA_NOPE:MLA_QK, :] * cs_r + qb_t[hd * MLA_ROPE:(hd + 1) * MLA_ROPE, :] * sn_r
        q_head = jnp.concatenate([qa[:MLA_NOPE, :], rope, qa[MLA_QK:, :]], axis=0)
        qt_out[sl, :] = (q_head * scale).astype(BF16)
        k_out[:, sl] = (kn[:, sl] + k_rope).astype(BF16)
    vt_out[...] = (_dot_nt(wuvt_ref[...], ckv) + vonec_ref[...]).astype(BF16)

    z = jax.nn.gelu(_dot(h, wsg_ref[...]))
    u = z[:, :SG_WIDTH]
    vn = _rms(z[:, SG_WIDTH:], gsg_ref[...]).astype(BF16)
    gw = SG_WIDTH // SG_GROUPS
    for n in range(tm // SG_CHUNK):
        rows = slice(n * SG_CHUNK, (n + 1) * SG_CHUNK)
        for g in range(SG_GROUPS):
            cols = slice(g * gw, (g + 1) * gw)
            sg = _dot(ws_ref[g], vn[rows, cols]) + bst_ref[:, g:g + 1]
            yb_out[rows, cols] = (u[rows, cols] * sg).astype(BF16)

    qx = _dot(h, wxq_ref[...]) * (X_HEAD_DIM ** -0.5)
    for hd in range(X_HEADS):
        sl = slice(hd * X_HEAD_DIM, (hd + 1) * X_HEAD_DIM)
        sc = _dot(qx[:, sl].astype(BF16), kmt_ref[sl, :])
        p = jnp.exp(sc - jnp.max(sc, axis=-1, keepdims=True))
        l = jnp.sum(p, axis=-1, keepdims=True)
        o = _dot(p.astype(BF16), vm_ref[:, sl])
        yc_out[:, sl] = (o / l).astype(BF16)


def _col_spec(height, tm):
    return pl.BlockSpec((height, tm), lambda i: (0, i))


def _mixer_pre(x, rope, lw, tm):
    s = x.shape[0]
    ins = [x, *rope, lw["g_mix"], lw["w_dq"], lw["w_dkv"], lw["w_kr"], lw["w_sg"],
           lw["w_xq"], lw["g_q"], lw["g_kv"], lw["w_uq_t"], lw["w_uqr_t"], lw["w_uk"],
           lw["w_uv_t"], lw["v_one"].T, lw["g_sg"], lw["w_s"],
           lw["b_st"], lw["k_mem_t"], lw["v_mem"]]
    in_specs = [_row_spec(tm, D_MODEL), _row_spec(tm, HEAD_PAD), _row_spec(tm, HEAD_PAD),
                _col_spec(HEAD_PAD, tm), _col_spec(HEAD_PAD, tm)] + \
               [_const_spec(a.shape) for a in ins[5:]]
    v_rows = MLA_HEADS * V_ROWS
    out_shape = [jax.ShapeDtypeStruct((QK_PAD, s), BF16), jax.ShapeDtypeStruct((s, QK_PAD), BF16),
                 jax.ShapeDtypeStruct((v_rows, s), BF16)] + \
                [jax.ShapeDtypeStruct((s, BRANCH_WIDTH), BF16)] * 2
    out_specs = [_col_spec(QK_PAD, tm), _row_spec(tm, QK_PAD), _col_spec(v_rows, tm)] + \
                [_row_spec(tm, BRANCH_WIDTH)] * 2
    return pl.pallas_call(
        _mixer_pre_kernel,
        grid=(s // tm,),
        in_specs=in_specs,
        out_specs=out_specs,
        out_shape=out_shape,
        compiler_params=_params("arbitrary"),
        name="mixer_pre",
    )(*ins)


def _flash_kernel(qt_ref, k_ref, vt_ref, o_ref, s_scr, m_scr, acc_scr, *, tk):
    n = k_ref.shape[0] // tk
    slots = FLASH_AHEAD + 1
    qt = qt_ref[...]

    def scores(t, slot):
        start = pl.multiple_of(t * tk, tk)
        s_scr[slot] = _dot(k_ref[pl.ds(start, tk), :], qt)

    def update(t, slot):
        start = pl.multiple_of(t * tk, tk)
        sc = s_scr[slot]
        m = m_scr[...]
        m_new = jnp.maximum(m, jnp.max(sc, axis=0, keepdims=True))
        alpha = jnp.exp2(m - m_new)
        p = jnp.exp2(sc - m_new).astype(BF16)
        acc_scr[...] = acc_scr[...] * alpha + _dot(vt_ref[:, pl.ds(start, tk)], p)
        m_scr[...] = m_new

    m_scr[...] = jnp.full(m_scr.shape, -jnp.inf, F32)
    acc_scr[...] = jnp.zeros(acc_scr.shape, F32)
    for t in range(min(FLASH_AHEAD, n)):
        scores(t, t % slots)
    loops = max(n - FLASH_AHEAD, 0) // FLASH_UNROLL

    def body(u, carry):
        base = u * FLASH_UNROLL
        for r in range(FLASH_UNROLL):
            scores(base + r + FLASH_AHEAD, (r + FLASH_AHEAD) % slots)
            update(base + r, r % slots)
        return carry

    lax.fori_loop(0, loops, body, 0)
    for t in range(loops * FLASH_UNROLL, n):
        if t + FLASH_AHEAD < n:
            scores(t + FLASH_AHEAD, (t + FLASH_AHEAD) % slots)
        update(t, t % slots)
    acc = acc_scr[...]
    out_t = acc / acc[MLA_V:MLA_V + 1, :]
    pad = jnp.zeros((HEAD_PAD - V_ROWS, out_t.shape[1]), F32)
    o_ref[...] = jnp.concatenate([out_t, pad], axis=0).T.astype(BF16)


def _flash(qt, k, vt, tq, tk):
    s = k.shape[0]
    return pl.pallas_call(
        functools.partial(_flash_kernel, tk=tk),
        grid=(MLA_HEADS, s // tq),
        in_specs=[pl.BlockSpec((HEAD_PAD, tq), lambda h, i: (h, i)),
                  pl.BlockSpec((s, HEAD_PAD), lambda h, i: (0, h)),
                  pl.BlockSpec((V_ROWS, s), lambda h, i: (h, 0))],
        out_specs=pl.BlockSpec((tq, HEAD_PAD), lambda h, i: (i, h)),
        out_shape=jax.ShapeDtypeStruct((s, QK_PAD), BF16),
        scratch_shapes=[pltpu.VMEM((FLASH_AHEAD + 1, tk, tq), F32), pltpu.VMEM((1, tq), F32),
                        pltpu.VMEM((V_ROWS, tq), F32)],
        compiler_params=_params("arbitrary", "arbitrary"),
        name="flash",
    )(qt, k, vt)


def _mixer_post_kernel(x_ref, ya_ref, yb_ref, yc_ref, gmix_ref, wgate_ref, bgate_ref,
                       wbr0_ref, wbr1_ref, wbr2_ref, wout_ref, o_ref):
    x = x_ref[...]
    h = _rms(x, gmix_ref[...]).astype(BF16)
    mix = None
    for n, (y_ref, w_ref) in enumerate(((ya_ref, wbr0_ref), (yb_ref, wbr1_ref), (yc_ref, wbr2_ref))):
        cols = slice(n * D_MODEL, (n + 1) * D_MODEL)
        gate = jax.nn.sigmoid(_dot(h, wgate_ref[:, cols]) + bgate_ref[n:n + 1, :])
        term = gate * _dot(y_ref[...], w_ref[...])
        mix = term if mix is None else mix + term
    o_ref[...] = x + _dot(mix.astype(BF16), wout_ref[...])


def _mixer_post(x, ya, yb, yc, lw, tm):
    s = x.shape[0]
    consts = [lw["g_mix"], lw["w_gate"], lw["b_gate"], lw["w_br0"], lw["w_br1"], lw["w_br2"],
              lw["w_out"]]
    in_specs = [_row_spec(tm, D_MODEL), _row_spec(tm, QK_PAD), _row_spec(tm, BRANCH_WIDTH),
                _row_spec(tm, BRANCH_WIDTH)] + [_const_spec(a.shape) for a in consts]
    return pl.pallas_call(
        _mixer_post_kernel,
        grid=(s // tm,),
        in_specs=in_specs,
        out_specs=_row_spec(tm, D_MODEL),
        out_shape=jax.ShapeDtypeStruct((s, D_MODEL), F32),
        compiler_params=_params("arbitrary"),
        name="mixer_post",
    )(x, ya, yb, yc, *consts)


def _dense_ffn_kernel(x_ref, g_ref, w1_ref, w3_ref, w2_ref, o_ref):
    x = x_ref[...]
    h = _rms(x, g_ref[...]).astype(BF16)
    t = jax.nn.silu(_dot(h, w1_ref[...])) * _dot(h, w3_ref[...])
    o_ref[...] = x + _dot(t.astype(BF16), w2_ref[...])


def _dense_ffn(x, g, w1, w3, w2, tm):
    s = x.shape[0]
    consts = [g, w1, w3, w2]
    return pl.pallas_call(
        _dense_ffn_kernel,
        grid=(s // tm,),
        in_specs=[_row_spec(tm, D_MODEL)] + [_const_spec(a.shape) for a in consts],
        out_specs=_row_spec(tm, D_MODEL),
        out_shape=jax.ShapeDtypeStruct((s, D_MODEL), F32),
        compiler_params=_params("arbitrary"),
        name="dense_ffn",
    )(x, *consts)


def _route_kernel(x_ref, g_ref, wrt_ref, tri_ref, h_out, e_out, rank_out, wcol_out, cnt_out,
                  carry_scr):
    tm = x_ref.shape[0]

    @pl.when(pl.program_id(0) == 0)
    def _init():
        carry_scr[...] = jnp.zeros(carry_scr.shape, F32)

    h = _rms(x_ref[...], g_ref[...])
    h_out[...] = h
    logits = _dot_nt(wrt_ref[...], h.astype(BF16))[:N_EXPERTS, :]
    row = lax.broadcasted_iota(jnp.int32, logits.shape, 0)
    m1 = jnp.max(logits, axis=0, keepdims=True)
    i1 = jnp.min(jnp.where(logits == m1, row, N_EXPERTS), axis=0, keepdims=True)
    rest = jnp.where(row == i1, -jnp.inf, logits)
    m2 = jnp.max(rest, axis=0, keepdims=True)
    i2 = jnp.min(jnp.where(rest == m2, row, N_EXPERTS), axis=0, keepdims=True)
    e2 = jnp.exp(m2 - m1)
    den = 1.0 + e2
    oh1 = (row == i1).astype(F32)
    oh2 = (row == i2).astype(F32)
    cum = _dot(jnp.concatenate([oh1, oh2], axis=0).astype(BF16), tri_ref[...])
    tot = cum[:N_EXPERTS] + cum[N_EXPERTS:] + carry_scr[:, 0:1]
    rank1 = jnp.sum(oh1 * tot, axis=0, keepdims=True) - 1.0
    rank2 = jnp.sum(oh2 * tot, axis=0, keepdims=True) - 1.0
    e_out[...] = jnp.concatenate([i1, i2], axis=0)
    rank_out[...] = jnp.concatenate([rank1, rank2], axis=0).astype(jnp.int32)
    carry_scr[...] = jnp.broadcast_to(tot[:, tm - 1:tm], carry_scr.shape)
    cnt_out[...] = carry_scr[...]
    w_rows = jnp.concatenate([1.0 / den, e2 / den, jnp.zeros((LANE - 2, tm), F32)], axis=0)
    wcol_out[...] = w_rows.T


def _route(x, g, w_router_t, tm):
    s = x.shape[0]
    tri = (jnp.arange(tm)[:, None] <= jnp.arange(tm)[None, :]).astype(BF16)
    pair = lambda dt: jax.ShapeDtypeStruct((2, s), dt)
    return pl.pallas_call(
        _route_kernel,
        grid=(s // tm,),
        in_specs=[_row_spec(tm, D_MODEL), _const_spec(g.shape), _const_spec(w_router_t.shape),
                  _const_spec(tri.shape)],
        out_specs=[_row_spec(tm, D_MODEL), _col_spec(2, tm), _col_spec(2, tm), _row_spec(tm, LANE),
                   pl.BlockSpec((N_EXPERTS, LANE), lambda i: (0, 0))],
        out_shape=[jax.ShapeDtypeStruct((s, D_MODEL), F32), pair(jnp.int32), pair(jnp.int32),
                   jax.ShapeDtypeStruct((s, LANE), F32),
                   jax.ShapeDtypeStruct((N_EXPERTS, LANE), F32)],
        scratch_shapes=[pltpu.VMEM((N_EXPERTS, LANE), F32)],
        compiler_params=_params("arbitrary"),
        name="moe_route",
    )(x, g, w_router_t, tri)


def _row_copy(src_ref, src_row, dst_ref, dst_row, sem):
    return pltpu.make_async_copy(src_ref.at[pl.ds(src_row, 1)], dst_ref.at[pl.ds(dst_row, 1)], sem)


def _dispatch_kernel(pos_ref, h_ref, xs_in_ref, xs_ref, sem):
    del xs_in_ref
    tm = h_ref.shape[0]

    def issue(t, carry):
        for k in range(2):
            _row_copy(h_ref, t, xs_ref, pos_ref[k, t], sem).start()
        return carry

    lax.fori_loop(0, tm, issue, 0, unroll=8)
    for k in range(2):
        pltpu.make_async_copy(h_ref, xs_ref.at[pl.ds(0, tm)], sem).wait()


def _dispatch(pos, h, n_slots, tm):
    s = h.shape[0]
    xs0 = jnp.zeros((n_slots, D_MODEL), F32)
    return pl.pallas_call(
        _dispatch_kernel,
        grid=(s // tm,),
        in_specs=[pl.BlockSpec((2, tm), lambda i: (0, i), memory_space=pltpu.SMEM),
                  _row_spec(tm, D_MODEL), pl.BlockSpec(memory_space=pl.ANY)],
        out_specs=pl.BlockSpec(memory_space=pl.ANY),
        out_shape=jax.ShapeDtypeStruct((n_slots, D_MODEL), F32),
        scratch_shapes=[pltpu.SemaphoreType.DMA(())],
        input_output_aliases={2: 0},
        compiler_params=pltpu.CompilerParams(dimension_semantics=("arbitrary",),
                                             vmem_limit_bytes=VMEM_LIMIT, has_side_effects=True),
        name="moe_dispatch",
    )(pos, h, xs0)


def _expert_kernel(te_ref, nu_ref, xs_ref, w1_ref, w3_ref, w2_ref, ys_ref, h_scr):
    del te_ref
    c = pl.program_id(1)

    @pl.when(c == 0)
    def _load():
        h_scr[...] = xs_ref[...].astype(BF16)
        ys_ref[...] = jnp.zeros(ys_ref.shape, F32)

    @pl.when(pl.program_id(0) < nu_ref[0])
    def _ffn():
        h = h_scr[...]
        t = jax.nn.silu(_dot(h, w1_ref[...])) * _dot(h, w3_ref[...])
        ys_ref[...] += _dot(t.astype(BF16), w2_ref[...])


def _experts(tile_expert, n_used, xs, w1, w3, w2, tm, fc):
    n_slots = xs.shape[0]
    n_chunks = w1.shape[-1] // fc

    def chunk(i, c, nu):
        return jnp.where(i < nu[0], c, n_chunks - 1)

    grid_spec = pltpu.PrefetchScalarGridSpec(
        num_scalar_prefetch=2,
        grid=(n_slots // tm, n_chunks),
        in_specs=[
            pl.BlockSpec((tm, D_MODEL), lambda i, c, te, nu: (i, 0)),
            pl.BlockSpec((None, D_MODEL, fc), lambda i, c, te, nu: (te[i], 0, chunk(i, c, nu))),
            pl.BlockSpec((None, D_MODEL, fc), lambda i, c, te, nu: (te[i], 0, chunk(i, c, nu))),
            pl.BlockSpec((None, fc, D_MODEL), lambda i, c, te, nu: (te[i], chunk(i, c, nu), 0)),
        ],
        out_specs=pl.BlockSpec((tm, D_MODEL), lambda i, c, te, nu: (i, 0)),
        scratch_shapes=[pltpu.VMEM((tm, D_MODEL), BF16)],
    )
    return pl.pallas_call(
        _expert_kernel,
        grid_spec=grid_spec,
        out_shape=jax.ShapeDtypeStruct((n_slots, D_MODEL), F32),
        compiler_params=_params("arbitrary", "arbitrary"),
        name="moe_experts",
    )(tile_expert, n_used, xs, w1, w3, w2)


def _combine_kernel(pos_ref, x_ref, wcol_ref, ys_ref, o_ref, buf, sem):
    tm = x_ref.shape[0]

    def issue(t, carry):
        for k in range(2):
            _row_copy(ys_ref, pos_ref[k, t], buf.at[k], t, sem).start()
        return carry

    lax.fori_loop(0, tm, issue, 0, unroll=8)
    for k in range(2):
        pltpu.make_async_copy(ys_ref.at[pl.ds(0, tm)], buf.at[k], sem).wait()
    w = wcol_ref[...]
    o_ref[...] = x_ref[...] + w[:, 0:1] * buf[0] + w[:, 1:2] * buf[1]


def _combine(pos, x, wcol, ys, tm):
    s = x.shape[0]
    return pl.pallas_call(
        _combine_kernel,
        grid=(s // tm,),
        in_specs=[pl.BlockSpec((2, tm), lambda i: (0, i), memory_space=pltpu.SMEM),
                  _row_spec(tm, D_MODEL), _row_spec(tm, LANE), pl.BlockSpec(memory_space=pl.ANY)],
        out_specs=_row_spec(tm, D_MODEL),
        out_shape=jax.ShapeDtypeStruct((s, D_MODEL), F32),
        scratch_shapes=[pltpu.VMEM((2, tm, D_MODEL), F32), pltpu.SemaphoreType.DMA(())],
        compiler_params=_params("arbitrary"),
        name="moe_combine",
    )(pos, x, wcol, ys)


def _moe(x, g, w_router, w1, w3, w2, tm, fc):
    s = x.shape[0]
    n_tiles = 2 * s // tm + N_EXPERTS
    wrt = jnp.pad(w_router.T, ((0, 16 - N_EXPERTS), (0, 0))).astype(BF16)
    h, e, rank, wcol, cnt = _route(x, g, wrt, tm)
    counts = cnt[:, 0].astype(jnp.int32)
    padded = (counts + tm - 1) // tm * tm
    ends = jnp.cumsum(padded)
    offs = ends - padded
    pos = rank
    for ex in range(N_EXPERTS):
        pos = pos + jnp.where(e == ex, offs[ex], 0)
    tile_start = jnp.arange(n_tiles, dtype=jnp.int32) * tm
    tile_expert = jnp.minimum(jnp.sum(tile_start[:, None] >= ends[None, :], axis=1),
                              N_EXPERTS - 1).astype(jnp.int32)
    n_used = (ends[-1:] // tm).astype(jnp.int32)
    xs = _dispatch(pos, h, n_tiles * tm, tm)
    ys = _experts(tile_expert, n_used, xs, w1, w3, w2, tm, fc)
    return _combine(pos, x, wcol, ys, tm)


def _final_norm_kernel(x_ref, g_ref, o_ref):
    o_ref[...] = _rms(x_ref[...], g_ref[...])


def _final_norm(x, g, tm):
    s = x.shape[0]
    return pl.pallas_call(
        _final_norm_kernel,
        grid=(s // tm,),
        in_specs=[_row_spec(tm, D_MODEL), _const_spec(g.shape)],
        out_specs=_row_spec(tm, D_MODEL),
        out_shape=jax.ShapeDtypeStruct((s, D_MODEL), F32),
        compiler_params=_params("arbitrary"),
        name="final_norm",
    )(x, g)


def _pad_heads(w, width, padded=HEAD_PAD):
    r = w.shape[0]
    w = w.reshape(r, MLA_HEADS, width)
    return jnp.pad(w, ((0, 0), (0, 0), (0, padded - width))).reshape(r, MLA_HEADS * padded)


def _rotate_half_cols(w_rope):
    half = MLA_ROPE // 2
    return jnp.concatenate([-w_rope[..., half:], w_rope[..., :half]], axis=-1)


def _layer_weights(l, g_mix, w_in, g_q, g_kv, w_uq, w_uk, w_uv, g_sg, w_s, b_s, w_br, b_gate,
                   w_out):
    o_dkv = Q_RANK
    o_kr = o_dkv + KV_RANK
    o_sg = o_kr + MLA_ROPE
    o_xq = o_sg + 2 * SG_WIDTH
    o_gate = o_xq + X_HEADS * X_HEAD_DIM
    wi = w_in[l]
    w_kr = wi[:, o_kr:o_sg]
    rope_pad = ((0, 0), (MLA_NOPE, HEAD_PAD - MLA_QK))
    w_kr2 = jnp.concatenate([jnp.pad(w_kr, rope_pad), jnp.pad(_rotate_half_cols(w_kr), rope_pad)],
                            axis=1)
    uq = w_uq[l].reshape(Q_RANK, MLA_HEADS, MLA_QK)
    uq_rot = _rotate_half_cols(uq[..., MLA_NOPE:]).reshape(Q_RANK, MLA_HEADS * MLA_ROPE)
    w_br0 = jnp.pad(w_br[l, 0].reshape(MLA_HEADS, MLA_V, D_MODEL),
                    ((0, 0), (0, HEAD_PAD - MLA_V), (0, 0))).reshape(QK_PAD, D_MODEL)
    return {
        "g_mix": g_mix[l][None, :],
        "w_dq": wi[:, :o_dkv].astype(BF16),
        "w_dkv": wi[:, o_dkv:o_kr].astype(BF16),
        "w_kr": w_kr2.astype(BF16),
        "w_sg": wi[:, o_sg:o_xq].astype(BF16),
        "w_xq": wi[:, o_xq:o_gate].astype(BF16),
        "w_gate": wi[:, o_gate:].astype(BF16),
        "g_q": g_q[l][None, :],
        "g_kv": g_kv[l][None, :],
        "w_uq_t": _pad_heads(w_uq[l], MLA_QK).T.astype(BF16),
        "w_uqr_t": uq_rot.T.astype(BF16),
        "w_uk": _pad_heads(w_uk[l], MLA_NOPE).astype(BF16),
        "w_uv_t": _pad_heads(w_uv[l], MLA_V, V_ROWS).T.astype(BF16),
        "g_sg": g_sg[l][None, :],
        "w_s": w_s[l].astype(BF16),
        "b_st": b_s[l].T,
        "b_gate": b_gate[l],
        "w_br0": w_br0.astype(BF16),
        "w_br1": w_br[l, 1].astype(BF16),
        "w_br2": w_br[l, 2].astype(BF16),
        "w_out": w_out[l].astype(BF16),
    }


def _rope_constants():
    half = MLA_ROPE // 2
    inv_freq = 1.0 / (ROPE_THETA ** (jnp.arange(0, MLA_ROPE, 2, dtype=F32) / MLA_ROPE))
    lane_freq = jnp.zeros((HEAD_PAD,), F32)
    lane_freq = lane_freq.at[MLA_NOPE:MLA_NOPE + half].set(inv_freq)
    lane_freq = lane_freq.at[MLA_NOPE + half:MLA_QK].set(inv_freq)
    v_one = jnp.zeros((MLA_HEADS, V_ROWS), F32).at[:, MLA_V].set(1.0).reshape(1, MLA_HEADS * V_ROWS)
    return lane_freq[None, :], v_one


def _tiles(s):
    tm = min(512, s)
    tq = min(512, s)
    tk = min(512, s)
    tmoe = min(512, s)
    return tm, tq, tk, tmoe


def kernel(x, mem, positions, g_mix, w_in, g_q, g_kv, w_uq, w_uk, w_uv, g_sg, w_s, b_s, g_mem, w_mkv, w_br, b_gate, w_out, g_ffn, dense_w1, dense_w3, dense_w2, w_router, moe_w1, moe_w3, moe_w2, g_final):
    b, s, d = x.shape
    assert b == 1 and d == D_MODEL and s % SG_CHUNK == 0
    depth = g_mix.shape[0]
    tm, tq, tk, tmoe = _tiles(s)
    xs = x[0]
    pos = positions[0].astype(F32)[:, None]
    inv_freq, v_one = _rope_constants()
    rope = _rope_tables(pos, inv_freq, tm)
    for l in range(depth):
        lw = _layer_weights(l, g_mix, w_in, g_q, g_kv, w_uq, w_uk, w_uv, g_sg, w_s, b_s, w_br,
                            b_gate, w_out)
        kv_mem = _mem_kv(mem[0], g_mem[None, :], w_mkv[l].astype(BF16))
        lw["k_mem_t"] = kv_mem[:, :BRANCH_WIDTH].T
        lw["v_mem"] = kv_mem[:, BRANCH_WIDTH:]
        lw["v_one"] = v_one
        qt, k, vt, yb, yc = _mixer_pre(xs, rope, lw, tm)
        ya = _flash(qt, k, vt, tq, tk)
        xs = _mixer_post(xs, ya, yb, yc, lw, tm)
        j = l // 2
        if l % 2 == 0:
            xs = _dense_ffn(xs, g_ffn[l][None, :], dense_w1[j].astype(BF16),
                            dense_w3[j].astype(BF16), dense_w2[j].astype(BF16), tm)
        else:
            fc = moe_w1.shape[-1] // 2
            xs = _moe(xs, g_ffn[l][None, :], w_router[j], moe_w1[j].astype(BF16),
                      moe_w3[j].astype(BF16), moe_w2[j].astype(BF16), tmoe, fc)
    return _final_norm(xs, g_final[None, :], tm)[None]
```

```python
import functools

import jax
import jax.numpy as jnp
from jax import lax
from jax.experimental import pallas as pl
from jax.experimental.pallas import tpu as pltpu

D_MODEL = 1024
N_BRANCH = 3
BRANCH_WIDTH = 512
MLA_HEADS = 8
MLA_NOPE = 64
MLA_ROPE = 32
MLA_QK = MLA_NOPE + MLA_ROPE
MLA_V = BRANCH_WIDTH // MLA_HEADS
Q_RANK = 384
KV_RANK = 256
ROPE_THETA = 10000.0
SG_CHUNK = 128
SG_GROUPS = 4
SG_WIDTH = BRANCH_WIDTH
X_HEADS = 4
X_HEAD_DIM = BRANCH_WIDTH // X_HEADS
N_EXPERTS = 8
EPS = 1e-6
LOG2_E = 1.4426950408889634

LANE = 128
HEAD_PAD = LANE
QK_PAD = MLA_HEADS * HEAD_PAD
V_ROWS = 80
VMEM_LIMIT = 56 * 1024 * 1024
FLASH_QBLOCKS = 4
FLASH_UNROLL = 14

F32 = jnp.float32
BF16 = jnp.bfloat16


def _dot(a, b):
    return jnp.dot(a, b, preferred_element_type=F32)


def _dot_nt(a, b):
    return lax.dot_general(a, b, (((1,), (1,)), ((), ())), preferred_element_type=F32)


def _rms(x, g):
    return x * lax.rsqrt(jnp.mean(x * x, axis=-1, keepdims=True) + EPS) * g


def _const_spec(shape):
    nd = len(shape)
    return pl.BlockSpec(shape, lambda *_: (0,) * nd, pipeline_mode=pl.Buffered(1))


def _row_spec(tm, width):
    return pl.BlockSpec((tm, width), lambda i: (i, 0))


def _params(*sem):
    return pltpu.CompilerParams(dimension_semantics=sem, vmem_limit_bytes=VMEM_LIMIT)


def _mem_kv_kernel(mem_ref, g_ref, w_ref, o_ref):
    mem_n = _rms(mem_ref[...], g_ref[...]).astype(BF16)
    o_ref[...] = _dot(mem_n, w_ref[...]).astype(BF16)


def _mem_kv(mem, g_mem, w_mkv):
    m = mem.shape[0]
    return pl.pallas_call(
        _mem_kv_kernel,
        out_shape=jax.ShapeDtypeStruct((m, 2 * BRANCH_WIDTH), BF16),
        name="mem_kv",
    )(mem, g_mem, w_mkv)


def _rope_tables_kernel(pos_ref, posr_ref, invf_ref, invfc_ref, cs_out, sn_out, cst_out, snt_out):
    ang = pos_ref[...] * invf_ref[...]
    cs_out[...] = jnp.cos(ang)
    sn_out[...] = jnp.sin(ang)
    ang_t = invfc_ref[...] * posr_ref[...]
    cst_out[...] = jnp.cos(ang_t)
    snt_out[...] = jnp.sin(ang_t)


def _rope_tables(pos, inv_freq, tm):
    s = pos.shape[0]
    row = jax.ShapeDtypeStruct((s, HEAD_PAD), F32)
    col = jax.ShapeDtypeStruct((HEAD_PAD, s), F32)
    return pl.pallas_call(
        _rope_tables_kernel,
        grid=(s // tm,),
        in_specs=[_row_spec(tm, 1), pl.BlockSpec((1, tm), lambda i: (0, i)),
                  _const_spec(inv_freq.shape), _const_spec(inv_freq.T.shape)],
        out_specs=[_row_spec(tm, HEAD_PAD)] * 2 + [pl.BlockSpec((HEAD_PAD, tm), lambda i: (0, i))] * 2,
        out_shape=[row, row, col, col],
        compiler_params=_params("arbitrary"),
        name="rope_tables",
    )(pos, pos.reshape(1, s), inv_freq, inv_freq.T)


def _mixer_pre_kernel(x_ref, cs_ref, sn_ref, cst_ref, snt_ref, gmix_ref, wdq_ref, wdkv_ref,
                      wkr_ref, wsg_ref, wxq_ref, gq_ref, gkv_ref, wuqt_ref, wuqrt_ref, wuk_ref,
                      wuvt_ref, vonec_ref, gsg_ref, ws_ref, bst_ref, kmt_ref, vm_ref,
                      qt_out, k_out, vt_out, yb_out, yc_out):
    tm = x_ref.shape[0]
    h = _rms(x_ref[...], gmix_ref[...]).astype(BF16)

    cq = _rms(_dot(h, wdq_ref[...]), gq_ref[...]).astype(BF16)
    qa_t = _dot_nt(wuqt_ref[...], cq)
    qb_t = _dot_nt(wuqrt_ref[...], cq)
    ckv = _rms(_dot(h, wdkv_ref[...]), gkv_ref[...]).astype(BF16)
    kn = _dot(ckv, wuk_ref[...])
    kr = _dot(h, wkr_ref[...])
    k_rope = kr[:, :LANE] * cs_ref[...] + kr[:, LANE:] * sn_ref[...]
    scale = MLA_QK ** -0.5 * LOG2_E
    cs_r = cst_ref[MLA_NOPE:MLA_QK, :]
    sn_r = snt_ref[MLA_NOPE:MLA_QK, :]
    for hd in range(MLA_HEADS):
        sl = slice(hd * HEAD_PAD, (hd + 1) * HEAD_PAD)
        qa = qa_t[sl, :]
        rope = qa[MLA_NOPE:MLA_QK, :] * cs_r + qb_t[hd * MLA_ROPE:(hd + 1) * MLA_ROPE, :] * sn_r
        q_head = jnp.concatenate([qa[:MLA_NOPE, :], rope, qa[MLA_QK:, :]], axis=0)
        qt_out[sl, :] = (q_head * scale).astype(BF16)
        k_out[:, sl] = (kn[:, sl] + k_rope).astype(BF16)
    vt_out[...] = (_dot_nt(wuvt_ref[...], ckv) + vonec_ref[...]).astype(BF16)

    z = jax.nn.gelu(_dot(h, wsg_ref[...]))
    u = z[:, :SG_WIDTH]
    vn = _rms(z[:, SG_WIDTH:], gsg_ref[...]).astype(BF16)
    gw = SG_WIDTH // SG_GROUPS
    for n in range(tm // SG_CHUNK):
        rows = slice(n * SG_CHUNK, (n + 1) * SG_CHUNK)
        for g in range(SG_GROUPS):
            cols = slice(g * gw, (g + 1) * gw)
            sg = _dot(ws_ref[g], vn[rows, cols]) + bst_ref[:, g:g + 1]
            yb_out[rows, cols] = (u[rows, cols] * sg).astype(BF16)

    qx = _dot(h, wxq_ref[...]) * (X_HEAD_DIM ** -0.5)
    for hd in range(X_HEADS):
        sl = slice(hd * X_HEAD_DIM, (hd + 1) * X_HEAD_DIM)
        sc = _dot(qx[:, sl].astype(BF16), kmt_ref[sl, :])
        p = jnp.exp(sc - jnp.max(sc, axis=-1, keepdims=True))
        l = jnp.sum(p, axis=-1, keepdims=True)
        o = _dot(p.astype(BF16), vm_ref[:, sl])
        yc_out[:, sl] = (o / l).astype(BF16)


def _col_spec(height, tm):
    return pl.BlockSpec((height, tm), lambda i: (0, i))


def _mixer_pre(x, rope, lw, tm):
    s = x.shape[0]
    ins = [x, *rope, lw["g_mix"], lw["w_dq"], lw["w_dkv"], lw["w_kr"], lw["w_sg"],
           lw["w_xq"], lw["g_q"], lw["g_kv"], lw["w_uq_t"], lw["w_uqr_t"], lw["w_uk"],
           lw["w_uv_t"], lw["v_one"].T, lw["g_sg"], lw["w_s"],
           lw["b_st"], lw["k_mem_t"], lw["v_mem"]]
    in_specs = [_row_spec(tm, D_MODEL), _row_spec(tm, HEAD_PAD), _row_spec(tm, HEAD_PAD),
                _col_spec(HEAD_PAD, tm), _col_spec(HEAD_PAD, tm)] + \
               [_const_spec(a.shape) for a in ins[5:]]
    v_rows = MLA_HEADS * V_ROWS
    out_shape = [jax.ShapeDtypeStruct((QK_PAD, s), BF16), jax.ShapeDtypeStruct((s, QK_PAD), BF16),
                 jax.ShapeDtypeStruct((v_rows, s), BF16)] + \
                [jax.ShapeDtypeStruct((s, BRANCH_WIDTH), BF16)] * 2
    out_specs = [_col_spec(QK_PAD, tm), _row_spec(tm, QK_PAD), _col_spec(v_rows, tm)] + \
                [_row_spec(tm, BRANCH_WIDTH)] * 2
    return pl.pallas_call(
        _mixer_pre_kernel,
        grid=(s // tm,),
        in_specs=in_specs,
        out_specs=out_specs,
        out_shape=out_shape,
        compiler_params=_params("arbitrary"),
        name="mixer_pre",
    )(*ins)


def _flash_kernel(qt_ref, k_ref, vt_ref, o_ref, s_scr, m_scr, acc_scr, *, tq, tk):
    n = k_ref.shape[0] // tk
    groups = qt_ref.shape[1] // tq

    def scores(g, t, slot):
        start = pl.multiple_of(t * tk, tk)
        s_scr[slot] = _dot(k_ref[pl.ds(start, tk), :], qt_ref[:, g * tq:(g + 1) * tq])

    def update(g, t, slot):
        start = pl.multiple_of(t * tk, tk)
        sc = s_scr[slot]
        m = m_scr[g]
        m_new = jnp.maximum(m, jnp.max(sc, axis=0, keepdims=True))
        alpha = jnp.exp2(m - m_new)
        p = jnp.exp2(sc - m_new).astype(BF16)
        acc_scr[g] = acc_scr[g] * alpha + _dot(vt_ref[:, pl.ds(start, tk)], p)
        m_scr[g] = m_new

    def step(g, t, slot):
        if t + 1 < n:
            scores(g, t + 1, 1 - slot)
        elif g + 1 < groups:
            scores(g + 1, 0, 1 - slot)
        update(g, t, slot)

    def finish(g):
        acc = acc_scr[g]
        out_t = acc / acc[MLA_V:MLA_V + 1, :]
        pad = jnp.zeros((HEAD_PAD - V_ROWS, tq), F32)
        o_ref[g * tq:(g + 1) * tq, :] = jnp.concatenate([out_t, pad], axis=0).T.astype(BF16)

    m_scr[...] = jnp.full(m_scr.shape, -jnp.inf, F32)
    acc_scr[...] = jnp.zeros(acc_scr.shape, F32)
    scores(0, 0, 0)
    loops = max(n - 2, 0) // FLASH_UNROLL
    for g in range(groups):
        off = g * n
        step(g, 0, off % 2)
        if g > 0:
            finish(g - 1)

        def body(u, carry, g=g, off=off):
            base = 1 + u * FLASH_UNROLL
            for r in range(FLASH_UNROLL):
                scores(g, base + r + 1, (off + r) % 2)
                update(g, base + r, (off + 1 + r) % 2)
            return carry

        lax.fori_loop(0, loops, body, 0)
        for t in range(1 + loops * FLASH_UNROLL, n):
            step(g, t, (off + t) % 2)
    finish(groups - 1)


def _flash(qt, k, vt, tq, tk):
    s = k.shape[0]
    groups = FLASH_QBLOCKS if s % (FLASH_QBLOCKS * tq) == 0 else 1
    tqg = groups * tq
    return pl.pallas_call(
        functools.partial(_flash_kernel, tq=tq, tk=tk),
        grid=(MLA_HEADS, s // tqg),
        in_specs=[pl.BlockSpec((HEAD_PAD, tqg), lambda h, i: (h, i)),
                  pl.BlockSpec((s, HEAD_PAD), lambda h, i: (0, h)),
                  pl.BlockSpec((V_ROWS, s), lambda h, i: (h, 0))],
        out_specs=pl.BlockSpec((tqg, HEAD_PAD), lambda h, i: (i, h)),
        out_shape=jax.ShapeDtypeStruct((s, QK_PAD), BF16),
        scratch_shapes=[pltpu.VMEM((2, tk, tq), F32), pltpu.VMEM((groups, 1, tq), F32),
                        pltpu.VMEM((groups, V_ROWS, tq), F32)],
        compiler_params=_params("arbitrary", "arbitrary"),
        name="flash",
    )(qt, k, vt)


def _mixer_post_kernel(x_ref, ya_ref, yb_ref, yc_ref, gmix_ref, wgate_ref, bgate_ref,
                       wbr0_ref, wbr1_ref, wbr2_ref, wout_ref, o_ref):
    x = x_ref[...]
    h = _rms(x, gmix_ref[...]).astype(BF16)
    mix = None
    for n, (y_ref, w_ref) in enumerate(((ya_ref, wbr0_ref), (yb_ref, wbr1_ref), (yc_ref, wbr2_ref))):
        cols = slice(n * D_MODEL, (n + 1) * D_MODEL)
        gate = jax.nn.sigmoid(_dot(h, wgate_ref[:, cols]) + bgate_ref[n:n + 1, :])
        term = gate * _dot(y_ref[...], w_ref[...])
        mix = term if mix is None else mix + term
    o_ref[...] = x + _dot(mix.astype(BF16), wout_ref[...])


def _mixer_post(x, ya, yb, yc, lw, tm):
    s = x.shape[0]
    consts = [lw["g_mix"], lw["w_gate"], lw["b_gate"], lw["w_br0"], lw["w_br1"], lw["w_br2"],
              lw["w_out"]]
    in_specs = [_row_spec(tm, D_MODEL), _row_spec(tm, QK_PAD), _row_spec(tm, BRANCH_WIDTH),
                _row_spec(tm, BRANCH_WIDTH)] + [_const_spec(a.shape) for a in consts]
    return pl.pallas_call(
        _mixer_post_kernel,
        grid=(s // tm,),
        in_specs=in_specs,
        out_specs=_row_spec(tm, D_MODEL),
        out_shape=jax.ShapeDtypeStruct((s, D_MODEL), F32),
        compiler_params=_params("arbitrary"),
        name="mixer_post",
    )(x, ya, yb, yc, *consts)


def _dense_ffn_kernel(x_ref, g_ref, w1_ref, w3_ref, w2_ref, o_ref):
    x = x_ref[...]
    h = _rms(x, g_ref[...]).astype(BF16)
    t = jax.nn.silu(_dot(h, w1_ref[...])) * _dot(h, w3_ref[...])
    o_ref[...] = x + _dot(t.astype(BF16), w2_ref[...])


def _dense_ffn(x, g, w1, w3, w2, tm):
    s = x.shape[0]
    consts = [g, w1, w3, w2]
    return pl.pallas_call(
        _dense_ffn_kernel,
        grid=(s // tm,),
        in_specs=[_row_spec(tm, D_MODEL)] + [_const_spec(a.shape) for a in consts],
        out_specs=_row_spec(tm, D_MODEL),
        out_shape=jax.ShapeDtypeStruct((s, D_MODEL), F32),
        compiler_params=_params("arbitrary"),
        name="dense_ffn",
    )(x, *consts)


def _route_kernel(x_ref, g_ref, wrt_ref, tri_ref, h_out, e_out, rank_out, wcol_out, cnt_out,
                  carry_scr):
    tm = x_ref.shape[0]

    @pl.when(pl.program_id(0) == 0)
    def _init():
        carry_scr[...] = jnp.zeros(carry_scr.shape, F32)

    h = _rms(x_ref[...], g_ref[...])
    h_out[...] = h
    logits = _dot_nt(wrt_ref[...], h.astype(BF16))[:N_EXPERTS, :]
    row = lax.broadcasted_iota(jnp.int32, logits.shape, 0)
    m1 = jnp.max(logits, axis=0, keepdims=True)
    i1 = jnp.min(jnp.where(logits == m1, row, N_EXPERTS), axis=0, keepdims=True)
    rest = jnp.where(row == i1, -jnp.inf, logits)
    m2 = jnp.max(rest, axis=0, keepdims=True)
    i2 = jnp.min(jnp.where(rest == m2, row, N_EXPERTS), axis=0, keepdims=True)
    e2 = jnp.exp(m2 - m1)
    den = 1.0 + e2
    oh1 = (row == i1).astype(F32)
    oh2 = (row == i2).astype(F32)
    cum = _dot(jnp.concatenate([oh1, oh2], axis=0).astype(BF16), tri_ref[...])
    tot = cum[:N_EXPERTS] + cum[N_EXPERTS:] + carry_scr[:, 0:1]
    rank1 = jnp.sum(oh1 * tot, axis=0, keepdims=True) - 1.0
    rank2 = jnp.sum(oh2 * tot, axis=0, keepdims=True) - 1.0
    e_out[...] = jnp.concatenate([i1, i2], axis=0)
    rank_out[...] = jnp.concatenate([rank1, rank2], axis=0).astype(jnp.int32)
    carry_scr[...] = jnp.broadcast_to(tot[:, tm - 1:tm], carry_scr.shape)
    cnt_out[...] = carry_scr[...]
    w_rows = jnp.concatenate([1.0 / den, e2 / den, jnp.zeros((LANE - 2, tm), F32)], axis=0)
    wcol_out[...] = w_rows.T


def _route(x, g, w_router_t, tm):
    s = x.shape[0]
    tri = (jnp.arange(tm)[:, None] <= jnp.arange(tm)[None, :]).astype(BF16)
    pair = lambda dt: jax.ShapeDtypeStruct((2, s), dt)
    return pl.pallas_call(
        _route_kernel,
        grid=(s // tm,),
        in_specs=[_row_spec(tm, D_MODEL), _const_spec(g.shape), _const_spec(w_router_t.shape),
                  _const_spec(tri.shape)],
        out_specs=[_row_spec(tm, D_MODEL), _col_spec(2, tm), _col_spec(2, tm), _row_spec(tm, LANE),
                   pl.BlockSpec((N_EXPERTS, LANE), lambda i: (0, 0))],
        out_shape=[jax.ShapeDtypeStruct((s, D_MODEL), F32), pair(jnp.int32), pair(jnp.int32),
                   jax.ShapeDtypeStruct((s, LANE), F32),
                   jax.ShapeDtypeStruct((N_EXPERTS, LANE), F32)],
        scratch_shapes=[pltpu.VMEM((N_EXPERTS, LANE), F32)],
        compiler_params=_params("arbitrary"),
        name="moe_route",
    )(x, g, w_router_t, tri)


def _row_copy(src_ref, src_row, dst_ref, dst_row, sem):
    return pltpu.make_async_copy(src_ref.at[pl.ds(src_row, 1)], dst_ref.at[pl.ds(dst_row, 1)], sem)


def _dispatch_kernel(pos_ref, h_ref, xs_in_ref, xs_ref, sem):
    del xs_in_ref
    tm = h_ref.shape[0]

    def issue(t, carry):
        for k in range(2):
            _row_copy(h_ref, t, xs_ref, pos_ref[k, t], sem).start()
        return carry

    lax.fori_loop(0, tm, issue, 0, unroll=8)
    for k in range(2):
        pltpu.make_async_copy(h_ref, xs_ref.at[pl.ds(0, tm)], sem).wait()


def _dispatch(pos, h, n_slots, tm):
    s = h.shape[0]
    xs0 = jnp.zeros((n_slots, D_MODEL), F32)
    return pl.pallas_call(
        _dispatch_kernel,
        grid=(s // tm,),
        in_specs=[pl.BlockSpec((2, tm), lambda i: (0, i), memory_space=pltpu.SMEM),
                  _row_spec(tm, D_MODEL), pl.BlockSpec(memory_space=pl.ANY)],
        out_specs=pl.BlockSpec(memory_space=pl.ANY),
        out_shape=jax.ShapeDtypeStruct((n_slots, D_MODEL), F32),
        scratch_shapes=[pltpu.SemaphoreType.DMA(())],
        input_output_aliases={2: 0},
        compiler_params=pltpu.CompilerParams(dimension_semantics=("arbitrary",),
                                             vmem_limit_bytes=VMEM_LIMIT, has_side_effects=True),
        name="moe_dispatch",
    )(pos, h, xs0)


def _expert_kernel(te_ref, nu_ref, xs_ref, w1_ref, w3_ref, w2_ref, ys_ref, h_scr):
    del te_ref
    c = pl.program_id(1)

    @pl.when(c == 0)
    def _load():
        h_scr[...] = xs_ref[...].astype(BF16)
        ys_ref[...] = jnp.zeros(ys_ref.shape, F32)

    @pl.when(pl.program_id(0) < nu_ref[0])
    def _ffn():
        h = h_scr[...]
        t = jax.nn.silu(_dot(h, w1_ref[...])) * _dot(h, w3_ref[...])
        ys_ref[...] += _dot(t.astype(BF16), w2_ref[...])


def _experts(tile_expert, n_used, xs, w1, w3, w2, tm, fc):
    n_slots = xs.shape[0]
    n_chunks = w1.shape[-1] // fc

    def chunk(i, c, nu):
        return jnp.where(i < nu[0], c, n_chunks - 1)

    grid_spec = pltpu.PrefetchScalarGridSpec(
        num_scalar_prefetch=2,
        grid=(n_slots // tm, n_chunks),
        in_specs=[
            pl.BlockSpec((tm, D_MODEL), lambda i, c, te, nu: (i, 0)),
            pl.BlockSpec((None, D_MODEL, fc), lambda i, c, te, nu: (te[i], 0, chunk(i, c, nu))),
            pl.BlockSpec((None, D_MODEL, fc), lambda i, c, te, nu: (te[i], 0, chunk(i, c, nu))),
            pl.BlockSpec((None, fc, D_MODEL), lambda i, c, te, nu: (te[i], chunk(i, c, nu), 0)),
        ],
        out_specs=pl.BlockSpec((tm, D_MODEL), lambda i, c, te, nu: (i, 0)),
        scratch_shapes=[pltpu.VMEM((tm, D_MODEL), BF16)],
    )
    return pl.pallas_call(
        _expert_kernel,
        grid_spec=grid_spec,
        out_shape=jax.ShapeDtypeStruct((n_slots, D_MODEL), F32),
        compiler_params=_params("arbitrary", "arbitrary"),
        name="moe_experts",
    )(tile_expert, n_used, xs, w1, w3, w2)


def _combine_kernel(pos_ref, x_ref, wcol_ref, ys_ref, o_ref, buf, sem):
    tm = x_ref.shape[0]

    def issue(t, carry):
        for k in range(2):
            _row_copy(ys_ref, pos_ref[k, t], buf.at[k], t, sem).start()
        return carry

    lax.fori_loop(0, tm, issue, 0, unroll=8)
    for k in range(2):
        pltpu.make_async_copy(ys_ref.at[pl.ds(0, tm)], buf.at[k], sem).wait()
    w = wcol_ref[...]
    o_ref[...] = x_ref[...] + w[:, 0:1] * buf[0] + w[:, 1:2] * buf[1]


def _combine(pos, x, wcol, ys, tm):
    s = x.shape[0]
    return pl.pallas_call(
        _combine_kernel,
        grid=(s // tm,),
        in_specs=[pl.BlockSpec((2, tm), lambda i: (0, i), memory_space=pltpu.SMEM),
                  _row_spec(tm, D_MODEL), _row_spec(tm, LANE), pl.BlockSpec(memory_space=pl.ANY)],
        out_specs=_row_spec(tm, D_MODEL),
        out_shape=jax.ShapeDtypeStruct((s, D_MODEL), F32),
        scratch_shapes=[pltpu.VMEM((2, tm, D_MODEL), F32), pltpu.SemaphoreType.DMA(())],
        compiler_params=_params("arbitrary"),
        name="moe_combine",
    )(pos, x, wcol, ys)


def _moe(x, g, w_router, w1, w3, w2, tm, fc):
    s = x.shape[0]
    n_tiles = 2 * s // tm + N_EXPERTS
    wrt = jnp.pad(w_router.T, ((0, 16 - N_EXPERTS), (0, 0))).astype(BF16)
    h, e, rank, wcol, cnt = _route(x, g, wrt, tm)
    counts = cnt[:, 0].astype(jnp.int32)
    padded = (counts + tm - 1) // tm * tm
    ends = jnp.cumsum(padded)
    offs = ends - padded
    pos = rank
    for ex in range(N_EXPERTS):
        pos = pos + jnp.where(e == ex, offs[ex], 0)
    tile_start = jnp.arange(n_tiles, dtype=jnp.int32) * tm
    tile_expert = jnp.minimum(jnp.sum(tile_start[:, None] >= ends[None, :], axis=1),
                              N_EXPERTS - 1).astype(jnp.int32)
    n_used = (ends[-1:] // tm).astype(jnp.int32)
    xs = _dispatch(pos, h, n_tiles * tm, tm)
    ys = _experts(tile_expert, n_used, xs, w1, w3, w2, tm, fc)
    return _combine(pos, x, wcol, ys, tm)


def _final_norm_kernel(x_ref, g_ref, o_ref):
    o_ref[...] = _rms(x_ref[...], g_ref[...])


def _final_norm(x, g, tm):
    s = x.shape[0]
    return pl.pallas_call(
        _final_norm_kernel,
        grid=(s // tm,),
        in_specs=[_row_spec(tm, D_MODEL), _const_spec(g.shape)],
        out_specs=_row_spec(tm, D_MODEL),
        out_shape=jax.ShapeDtypeStruct((s, D_MODEL), F32),
        compiler_params=_params("arbitrary"),
        name="final_norm",
    )(x, g)


def _pad_heads(w, width, padded=HEAD_PAD):
    r = w.shape[0]
    w = w.reshape(r, MLA_HEADS, width)
    return jnp.pad(w, ((0, 0), (0, 0), (0, padded - width))).reshape(r, MLA_HEADS * padded)


def _rotate_half_cols(w_rope):
    half = MLA_ROPE // 2
    return jnp.concatenate([-w_rope[..., half:], w_rope[..., :half]], axis=-1)


def _layer_weights(l, g_mix, w_in, g_q, g_kv, w_uq, w_uk, w_uv, g_sg, w_s, b_s, w_br, b_gate,
                   w_out):
    o_dkv = Q_RANK
    o_kr = o_dkv + KV_RANK
    o_sg = o_kr + MLA_ROPE
    o_xq = o_sg + 2 * SG_WIDTH
    o_gate = o_xq + X_HEADS * X_HEAD_DIM
    wi = w_in[l]
    w_kr = wi[:, o_kr:o_sg]
    rope_pad = ((0, 0), (MLA_NOPE, HEAD_PAD - MLA_QK))
    w_kr2 = jnp.concatenate([jnp.pad(w_kr, rope_pad), jnp.pad(_rotate_half_cols(w_kr), rope_pad)],
                            axis=1)
    uq = w_uq[l].reshape(Q_RANK, MLA_HEADS, MLA_QK)
    uq_rot = _rotate_half_cols(uq[..., MLA_NOPE:]).reshape(Q_RANK, MLA_HEADS * MLA_ROPE)
    w_br0 = jnp.pad(w_br[l, 0].reshape(MLA_HEADS, MLA_V, D_MODEL),
                    ((0, 0), (0, HEAD_PAD - MLA_V), (0, 0))).reshape(QK_PAD, D_MODEL)
    return {
        "g_mix": g_mix[l][None, :],
        "w_dq": wi[:, :o_dkv].astype(BF16),
        "w_dkv": wi[:, o_dkv:o_kr].astype(BF16),
        "w_kr": w_kr2.astype(BF16),
        "w_sg": wi[:, o_sg:o_xq].astype(BF16),
        "w_xq": wi[:, o_xq:o_gate].astype(BF16),
        "w_gate": wi[:, o_gate:].astype(BF16),
        "g_q": g_q[l][None, :],
        "g_kv": g_kv[l][None, :],
        "w_uq_t": _pad_heads(w_uq[l], MLA_QK).T.astype(BF16),
        "w_uqr_t": uq_rot.T.astype(BF16),
        "w_uk": _pad_heads(w_uk[l], MLA_NOPE).astype(BF16),
        "w_uv_t": _pad_heads(w_uv[l], MLA_V, V_ROWS).T.astype(BF16),
        "g_sg": g_sg[l][None, :],
        "w_s": w_s[l].astype(BF16),
        "b_st": b_s[l].T,
        "b_gate": b_gate[l],
        "w_br0": w_br0.astype(BF16),
        "w_br1": w_br[l, 1].astype(BF16),
        "w_br2": w_br[l, 2].astype(BF16),
        "w_out": w_out[l].astype(BF16),
    }


def _rope_constants():
    half = MLA_ROPE // 2
    inv_freq = 1.0 / (ROPE_THETA ** (jnp.arange(0, MLA_ROPE, 2, dtype=F32) / MLA_ROPE))
    lane_freq = jnp.zeros((HEAD_PAD,), F32)
    lane_freq = lane_freq.at[MLA_NOPE:MLA_NOPE + half].set(inv_freq)
    lane_freq = lane_freq.at[MLA_NOPE + half:MLA_QK].set(inv_freq)
    v_one = jnp.zeros((MLA_HEADS, V_ROWS), F32).at[:, MLA_V].set(1.0).reshape(1, MLA_HEADS * V_ROWS)
    return lane_freq[None, :], v_one


def _tiles(s):
    tm = min(512, s)
    tq = min(512, s)
    tk = min(512, s)
    tmoe = min(512, s)
    return tm, tq, tk, tmoe


def kernel(x, mem, positions, g_mix, w_in, g_q, g_kv, w_uq, w_uk, w_uv, g_sg, w_s, b_s, g_mem, w_mkv, w_br, b_gate, w_out, g_ffn, dense_w1, dense_w3, dense_w2, w_router, moe_w1, moe_w3, moe_w2, g_final):
    b, s, d = x.shape
    assert b == 1 and d == D_MODEL and s % SG_CHUNK == 0
    depth = g_mix.shape[0]
    tm, tq, tk, tmoe = _tiles(s)
    xs = x[0]
    pos = positions[0].astype(F32)[:, None]
    inv_freq, v_one = _rope_constants()
    rope = _rope_tables(pos, inv_freq, tm)
    for l in range(depth):
        lw = _layer_weights(l, g_mix, w_in, g_q, g_kv, w_uq, w_uk, w_uv, g_sg, w_s, b_s, w_br,
                            b_gate, w_out)
        kv_mem = _mem_kv(mem[0], g_mem[None, :], w_mkv[l].astype(BF16))
        lw["k_mem_t"] = kv_mem[:, :BRANCH_WIDTH].T
        lw["v_mem"] = kv_mem[:, BRANCH_WIDTH:]
        lw["v_one"] = v_one
        qt, k, vt, yb, yc = _mixer_pre(xs, rope, lw, tm)
        ya = _flash(qt, k, vt, tq, tk)
        xs = _mixer_post(xs, ya, yb, yc, lw, tm)
        j = l // 2
        if l % 2 == 0:
            xs = _dense_ffn(xs, g_ffn[l][None, :], dense_w1[j].astype(BF16),
                            dense_w3[j].astype(BF16), dense_w2[j].astype(BF16), tm)
        else:
            fc = moe_w1.shape[-1] // 2
            xs = _moe(xs, g_ffn[l][None, :], w_router[j], moe_w1[j].astype(BF16),
                      moe_w3[j].astype(BF16), moe_w2[j].astype(BF16), tmoe, fc)
    return _final_norm(xs, g_final[None, :], tm)[None]
```

```python
import functools

import jax
import jax.numpy as jnp
from jax import lax
from jax.experimental import pallas as pl
from jax.experimental.pallas import tpu as pltpu

D_MODEL = 1024
N_BRANCH = 3
BRANCH_WIDTH = 512
MLA_HEADS = 8
MLA_NOPE = 64
MLA_ROPE = 32
MLA_QK = MLA_NOPE + MLA_ROPE
MLA_V = BRANCH_WIDTH // MLA_HEADS
Q_RANK = 384
KV_RANK = 256
ROPE_THETA = 10000.0
SG_CHUNK = 128
SG_GROUPS = 4
SG_WIDTH = BRANCH_WIDTH
X_HEADS = 4
X_HEAD_DIM = BRANCH_WIDTH // X_HEADS
N_EXPERTS = 8
EPS = 1e-6
LOG2_E = 1.4426950408889634

LANE = 128
HEAD_PAD = LANE
QK_PAD = MLA_HEADS * HEAD_PAD
V_ROWS = 80
VMEM_LIMIT = 56 * 1024 * 1024
FLASH_QBLOCKS = 4
FLASH_UNROLL = 14

F32 = jnp.float32
BF16 = jnp.bfloat16


def _dot(a, b):
    return jnp.dot(a, b, preferred_element_type=F32)


def _dot_nt(a, b):
    return lax.dot_general(a, b, (((1,), (1,)), ((), ())), preferred_element_type=F32)


def _rms(x, g):
    return x * lax.rsqrt(jnp.mean(x * x, axis=-1, keepdims=True) + EPS) * g


def _const_spec(shape):
    nd = len(shape)
    return pl.BlockSpec(shape, lambda *_: (0,) * nd, pipeline_mode=pl.Buffered(1))


def _row_spec(tm, width):
    return pl.BlockSpec((tm, width), lambda i: (i, 0))


def _params(*sem):
    return pltpu.CompilerParams(dimension_semantics=sem, vmem_limit_bytes=VMEM_LIMIT)


def _mem_kv_kernel(mem_ref, g_ref, w_ref, o_ref):
    mem_n = _rms(mem_ref[...], g_ref[...]).astype(BF16)
    o_ref[...] = _dot(mem_n, w_ref[...]).astype(BF16)


def _mem_kv(mem, g_mem, w_mkv):
    m = mem.shape[0]
    return pl.pallas_call(
        _mem_kv_kernel,
        out_shape=jax.ShapeDtypeStruct((m, 2 * BRANCH_WIDTH), BF16),
        name="mem_kv",
    )(mem, g_mem, w_mkv)


def _rope_tables_kernel(pos_ref, posr_ref, invf_ref, invfc_ref, cs_out, sn_out, cst_out, snt_out):
    ang = pos_ref[...] * invf_ref[...]
    cs_out[...] = jnp.cos(ang)
    sn_out[...] = jnp.sin(ang)
    ang_t = invfc_ref[...] * posr_ref[...]
    cst_out[...] = jnp.cos(ang_t)
    snt_out[...] = jnp.sin(ang_t)


def _rope_tables(pos, inv_freq, tm):
    s = pos.shape[0]
    row = jax.ShapeDtypeStruct((s, HEAD_PAD), F32)
    col = jax.ShapeDtypeStruct((HEAD_PAD, s), F32)
    return pl.pallas_call(
        _rope_tables_kernel,
        grid=(s // tm,),
        in_specs=[_row_spec(tm, 1), pl.BlockSpec((1, tm), lambda i: (0, i)),
                  _const_spec(inv_freq.shape), _const_spec(inv_freq.T.shape)],
        out_specs=[_row_spec(tm, HEAD_PAD)] * 2 + [pl.BlockSpec((HEAD_PAD, tm), lambda i: (0, i))] * 2,
        out_shape=[row, row, col, col],
        compiler_params=_params("arbitrary"),
        name="rope_tables",
    )(pos, pos.reshape(1, s), inv_freq, inv_freq.T)


def _mixer_pre_kernel(x_ref, cs_ref, sn_ref, cst_ref, snt_ref, gmix_ref, wdq_ref, wdkv_ref,
                      wkr_ref, wsg_ref, wxq_ref, gq_ref, gkv_ref, wuqt_ref, wuqrt_ref, wuk_ref,
                      wuvt_ref, vonec_ref, gsg_ref, ws_ref, bst_ref, kmt_ref, vm_ref,
                      qt_out, k_out, vt_out, yb_out, yc_out):
    tm = x_ref.shape[0]
    h = _rms(x_ref[...], gmix_ref[...]).astype(BF16)

    cq = _rms(_dot(h, wdq_ref[...]), gq_ref[...]).astype(BF16)
    qa_t = _dot_nt(wuqt_ref[...], cq)
    qb_t = _dot_nt(wuqrt_ref[...], cq)
    ckv = _rms(_dot(h, wdkv_ref[...]), gkv_ref[...]).astype(BF16)
    kn = _dot(ckv, wuk_ref[...])
    kr = _dot(h, wkr_ref[...])
    k_rope = kr[:, :LANE] * cs_ref[...] + kr[:, LANE:] * sn_ref[...]
    scale = MLA_QK ** -0.5 * LOG2_E
    cs_r = cst_ref[MLA_NOPE:MLA_QK, :]
    sn_r = snt_ref[MLA_NOPE:MLA_QK, :]
    for hd in range(MLA_HEADS):
        sl = slice(hd * HEAD_PAD, (hd + 1) * HEAD_PAD)
        qa = qa_t[sl, :]
        rope = qa[MLA_NOPE:MLA_QK, :] * cs_r + qb_t[hd * MLA_ROPE:(hd + 1) * MLA_ROPE, :] * sn_r
        q_head = jnp.concatenate([qa[:MLA_NOPE, :], rope, qa[MLA_QK:, :]], axis=0)
        qt_out[sl, :] = (q_head * scale).astype(BF16)
        k_out[:, sl] = (kn[:, sl] + k_rope).astype(BF16)
    vt_out[...] = (_dot_nt(wuvt_ref[...], ckv) + vonec_ref[...]).astype(BF16)

    z = jax.nn.gelu(_dot(h, wsg_ref[...]))
    u = z[:, :SG_WIDTH]
    vn = _rms(z[:, SG_WIDTH:], gsg_ref[...]).astype(BF16)
    gw = SG_WIDTH // SG_GROUPS
    for n in range(tm // SG_CHUNK):
        rows = slice(n * SG_CHUNK, (n + 1) * SG_CHUNK)
        for g in range(SG_GROUPS):
            cols = slice(g * gw, (g + 1) * gw)
            sg = _dot(ws_ref[g], vn[rows, cols]) + bst_ref[:, g:g + 1]
            yb_out[rows, cols] = (u[rows, cols] * sg).astype(BF16)

    qx = _dot(h, wxq_ref[...]) * (X_HEAD_DIM ** -0.5)
    for hd in range(X_HEADS):
        sl = slice(hd * X_HEAD_DIM, (hd + 1) * X_HEAD_DIM)
        sc = _dot(qx[:, sl].astype(BF16), kmt_ref[sl, :])
        p = jnp.exp(sc - jnp.max(sc, axis=-1, keepdims=True))
        l = jnp.sum(p, axis=-1, keepdims=True)
        o = _dot(p.astype(BF16), vm_ref[:, sl])
        yc_out[:, sl] = (o / l).astype(BF16)


def _col_spec(height, tm):
    return pl.BlockSpec((height, tm), lambda i: (0, i))


def _mixer_pre(x, rope, lw, tm):
    s = x.shape[0]
    ins = [x, *rope, lw["g_mix"], lw["w_dq"], lw["w_dkv"], lw["w_kr"], lw["w_sg"],
           lw["w_xq"], lw["g_q"], lw["g_kv"], lw["w_uq_t"], lw["w_uqr_t"], lw["w_uk"],
           lw["w_uv_t"], lw["v_one"].T, lw["g_sg"], lw["w_s"],
           lw["b_st"], lw["k_mem_t"], lw["v_mem"]]
    in_specs = [_row_spec(tm, D_MODEL), _row_spec(tm, HEAD_PAD), _row_spec(tm, HEAD_PAD),
                _col_spec(HEAD_PAD, tm), _col_spec(HEAD_PAD, tm)] + \
               [_const_spec(a.shape) for a in ins[5:]]
    v_rows = MLA_HEADS * V_ROWS
    out_shape = [jax.ShapeDtypeStruct((QK_PAD, s), BF16), jax.ShapeDtypeStruct((s, QK_PAD), BF16),
                 jax.ShapeDtypeStruct((v_rows, s), BF16)] + \
                [jax.ShapeDtypeStruct((s, BRANCH_WIDTH), BF16)] * 2
    out_specs = [_col_spec(QK_PAD, tm), _row_spec(tm, QK_PAD), _col_spec(v_rows, tm)] + \
                [_row_spec(tm, BRANCH_WIDTH)] * 2
    return pl.pallas_call(
        _mixer_pre_kernel,
        grid=(s // tm,),
        in_specs=in_specs,
        out_specs=out_specs,
        out_shape=out_shape,
        compiler_params=_params("arbitrary"),
        name="mixer_pre",
    )(*ins)


def _flash_kernel(qt_ref, k_ref, vt_ref, o_ref, s_scr, m_scr, acc_scr, *, tq, tk):
    n = k_ref.shape[0] // tk
    groups = qt_ref.shape[1] // tq

    def scores(g, t, slot):
        start = pl.multiple_of(t * tk, tk)
        s_scr[slot] = _dot(k_ref[pl.ds(start, tk), :], qt_ref[:, g * tq:(g + 1) * tq])

    def update(g, t, slot):
        start = pl.multiple_of(t * tk, tk)
        sc = s_scr[slot]
        m = m_scr[g]
        m_new = jnp.maximum(m, jnp.max(sc, axis=0, keepdims=True))
        alpha = jnp.exp2(m - m_new)
        p = jnp.exp2(sc - m_new).astype(BF16)
        acc_scr[g] = acc_scr[g] * alpha + _dot(vt_ref[:, pl.ds(start, tk)], p)
        m_scr[g] = m_new

    def step(g, t, slot):
        if t + 1 < n:
            scores(g, t + 1, 1 - slot)
        elif g + 1 < groups:
            scores(g + 1, 0, 1 - slot)
        update(g, t, slot)

    def finish(g):
        acc = acc_scr[g]
        out_t = acc / acc[MLA_V:MLA_V + 1, :]
        pad = jnp.zeros((HEAD_PAD - V_ROWS, tq), F32)
        o_ref[g * tq:(g + 1) * tq, :] = jnp.concatenate([out_t, pad], axis=0).T.astype(BF16)

    m_scr[...] = jnp.full(m_scr.shape, -jnp.inf, F32)
    acc_scr[...] = jnp.zeros(acc_scr.shape, F32)
    scores(0, 0, 0)
    loops = max(n - 2, 0) // FLASH_UNROLL
    for g in range(groups):
        off = g * n
        step(g, 0, off % 2)
        if g > 0:
            finish(g - 1)

        def body(u, carry, g=g, off=off):
            base = 1 + u * FLASH_UNROLL
            for r in range(FLASH_UNROLL):
                scores(g, base + r + 1, (off + r) % 2)
                update(g, base + r, (off + 1 + r) % 2)
            return carry

        lax.fori_loop(0, loops, body, 0)
        for t in range(1 + loops * FLASH_UNROLL, n):
            step(g, t, (off + t) % 2)
    finish(groups - 1)


def _flash(qt, k, vt, tq, tk):
    s = k.shape[0]
    groups = FLASH_QBLOCKS if s % (FLASH_QBLOCKS * tq) == 0 else 1
    tqg = groups * tq
    return pl.pallas_call(
        functools.partial(_flash_kernel, tq=tq, tk=tk),
        grid=(MLA_HEADS, s // tqg),
        in_specs=[pl.BlockSpec((HEAD_PAD, tqg), lambda h, i: (h, i)),
                  pl.BlockSpec((s, HEAD_PAD), lambda h, i: (0, h)),
                  pl.BlockSpec((V_ROWS, s), lambda h, i: (h, 0))],
        out_specs=pl.BlockSpec((tqg, HEAD_PAD), lambda h, i: (i, h)),
        out_shape=jax.ShapeDtypeStruct((s, QK_PAD), BF16),
        scratch_shapes=[pltpu.VMEM((2, tk, tq), F32), pltpu.VMEM((groups, 1, tq), F32),
                        pltpu.VMEM((groups, V_ROWS, tq), F32)],
        compiler_params=_params("arbitrary", "arbitrary"),
        name="flash",
    )(qt, k, vt)


def _mixer_post_kernel(x_ref, ya_ref, yb_ref, yc_ref, gmix_ref, wgate_ref, bgate_ref,
                       wbr0_ref, wbr1_ref, wbr2_ref, wout_ref, o_ref):
    x = x_ref[...]
    h = _rms(x, gmix_ref[...]).astype(BF16)
    mix = None
    for n, (y_ref, w_ref) in enumerate(((ya_ref, wbr0_ref), (yb_ref, wbr1_ref), (yc_ref, wbr2_ref))):
        cols = slice(n * D_MODEL, (n + 1) * D_MODEL)
        gate = jax.nn.sigmoid(_dot(h, wgate_ref[:, cols]) + bgate_ref[n:n + 1, :])
        term = gate * _dot(y_ref[...], w_ref[...])
        mix = term if mix is None else mix + term
    o_ref[...] = x + _dot(mix.astype(BF16), wout_ref[...])


def _mixer_post(x, ya, yb, yc, lw, tm):
    s = x.shape[0]
    consts = [lw["g_mix"], lw["w_gate"], lw["b_gate"], lw["w_br0"], lw["w_br1"], lw["w_br2"],
              lw["w_out"]]
    in_specs = [_row_spec(tm, D_MODEL), _row_spec(tm, QK_PAD), _row_spec(tm, BRANCH_WIDTH),
                _row_spec(tm, BRANCH_WIDTH)] + [_const_spec(a.shape) for a in consts]
    return pl.pallas_call(
        _mixer_post_kernel,
        grid=(s // tm,),
        in_specs=in_specs,
        out_specs=_row_spec(tm, D_MODEL),
        out_shape=jax.ShapeDtypeStruct((s, D_MODEL), F32),
        compiler_params=_params("arbitrary"),
        name="mixer_post",
    )(x, ya, yb, yc, *consts)


def _dense_ffn_kernel(x_ref, g_ref, w1_ref, w3_ref, w2_ref, o_ref):
    x = x_ref[...]
    h = _rms(x, g_ref[...]).astype(BF16)
    t = jax.nn.silu(_dot(h, w1_ref[...])) * _dot(h, w3_ref[...])
    o_ref[...] = x + _dot(t.astype(BF16), w2_ref[...])


def _dense_ffn(x, g, w1, w3, w2, tm):
    s = x.shape[0]
    consts = [g, w1, w3, w2]
    return pl.pallas_call(
        _dense_ffn_kernel,
        grid=(s // tm,),
        in_specs=[_row_spec(tm, D_MODEL)] + [_const_spec(a.shape) for a in consts],
        out_specs=_row_spec(tm, D_MODEL),
        out_shape=jax.ShapeDtypeStruct((s, D_MODEL), F32),
        compiler_params=_params("arbitrary"),
        name="dense_ffn",
    )(x, *consts)


def _route_kernel(x_ref, g_ref, wrt_ref, tri_ref, h_out, e_out, rank_out, wcol_out, cnt_out,
                  carry_scr):
    tm = x_ref.shape[0]

    @pl.when(pl.program_id(0) == 0)
    def _init():
        carry_scr[...] = jnp.zeros(carry_scr.shape, F32)

    h = _rms(x_ref[...], g_ref[...])
    h_out[...] = h
    logits = _dot_nt(wrt_ref[...], h.astype(BF16))[:N_EXPERTS, :]
    row = lax.broadcasted_iota(jnp.int32, logits.shape, 0)
    m1 = jnp.max(logits, axis=0, keepdims=True)
    i1 = jnp.min(jnp.where(logits == m1, row, N_EXPERTS), axis=0, keepdims=True)
    rest = jnp.where(row == i1, -jnp.inf, logits)
    m2 = jnp.max(rest, axis=0, keepdims=True)
    i2 = jnp.min(jnp.where(rest == m2, row, N_EXPERTS), axis=0, keepdims=True)
    e2 = jnp.exp(m2 - m1)
    den = 1.0 + e2
    oh1 = (row == i1).astype(F32)
    oh2 = (row == i2).astype(F32)
    cum = _dot(jnp.concatenate([oh1, oh2], axis=0).astype(BF16), tri_ref[...])
    tot = cum[:N_EXPERTS] + cum[N_EXPERTS:] + carry_scr[:, 0:1]
    rank1 = jnp.sum(oh1 * tot, axis=0, keepdims=True) - 1.0
    rank2 = jnp.sum(oh2 * tot, axis=0, keepdims=True) - 1.0
    e_out[...] = jnp.concatenate([i1, i2], axis=0)
    rank_out[...] = jnp.concatenate([rank1, rank2], axis=0).astype(jnp.int32)
    carry_scr[...] = jnp.broadcast_to(tot[:, tm - 1:tm], carry_scr.shape)
    cnt_out[...] = carry_scr[...]
    w_rows = jnp.concatenate([1.0 / den, e2 / den, jnp.zeros((LANE - 2, tm), F32)], axis=0)
    wcol_out[...] = w_rows.T


def _route(x, g, w_router_t, tm):
    s = x.shape[0]
    tri = (jnp.arange(tm)[:, None] <= jnp.arange(tm)[None, :]).astype(BF16)
    pair = lambda dt: jax.ShapeDtypeStruct((2, s), dt)
    return pl.pallas_call(
        _route_kernel,
        grid=(s // tm,),
        in_specs=[_row_spec(tm, D_MODEL), _const_spec(g.shape), _const_spec(w_router_t.shape),
                  _const_spec(tri.shape)],
        out_specs=[_row_spec(tm, D_MODEL), _col_spec(2, tm), _col_spec(2, tm), _row_spec(tm, LANE),
                   pl.BlockSpec((N_EXPERTS, LANE), lambda i: (0, 0))],
        out_shape=[jax.ShapeDtypeStruct((s, D_MODEL), F32), pair(jnp.int32), pair(jnp.int32),
                   jax.ShapeDtypeStruct((s, LANE), F32),
                   jax.ShapeDtypeStruct((N_EXPERTS, LANE), F32)],
        scratch_shapes=[pltpu.VMEM((N_EXPERTS, LANE), F32)],
        compiler_params=_params("arbitrary"),
        name="moe_route",
    )(x, g, w_router_t, tri)


def _row_copy(src_ref, src_row, dst_ref, dst_row, sem):
    return pltpu.make_async_copy(src_ref.at[pl.ds(src_row, 1)], dst_ref.at[pl.ds(dst_row, 1)], sem)


def _dispatch_kernel(pos_ref, h_ref, xs_in_ref, xs_ref, sem):
    del xs_in_ref
    tm = h_ref.shape[0]

    def issue(t, carry):
        for k in range(2):
            _row_copy(h_ref, t, xs_ref, pos_ref[k, t], sem).start()
        return carry

    lax.fori_loop(0, tm, issue, 0, unroll=8)
    for k in range(2):
        pltpu.make_async_copy(h_ref, xs_ref.at[pl.ds(0, tm)], sem).wait()


def _dispatch(pos, h, n_slots, tm):
    s = h.shape[0]
    xs0 = jnp.zeros((n_slots, D_MODEL), F32)
    return pl.pallas_call(
        _dispatch_kernel,
        grid=(s // tm,),
        in_specs=[pl.BlockSpec((2, tm), lambda i: (0, i), memory_space=pltpu.SMEM),
                  _row_spec(tm, D_MODEL), pl.BlockSpec(memory_space=pl.ANY)],
        out_specs=pl.BlockSpec(memory_space=pl.ANY),
        out_shape=jax.ShapeDtypeStruct((n_slots, D_MODEL), F32),
        scratch_shapes=[pltpu.SemaphoreType.DMA(())],
        input_output_aliases={2: 0},
        compiler_params=pltpu.CompilerParams(dimension_semantics=("arbitrary",),
                                             vmem_limit_bytes=VMEM_LIMIT, has_side_effects=True),
        name="moe_dispatch",
    )(pos, h, xs0)


def _expert_kernel(te_ref, nu_ref, xs_ref, w1_ref, w3_ref, w2_ref, ys_ref, h_scr):
    del te_ref
    c = pl.program_id(1)

    @pl.when(c == 0)
    def _load():
        h_scr[...] = xs_ref[...].astype(BF16)
        ys_ref[...] = jnp.zeros(ys_ref.shape, F32)

    @pl.when(pl.program_id(0) < nu_ref[0])
    def _ffn():
        h = h_scr[...]
        t = jax.nn.silu(_dot(h, w1_ref[...])) * _dot(h, w3_ref[...])
        ys_ref[...] += _dot(t.astype(BF16), w2_ref[...])


def _experts(tile_expert, n_used, xs, w1, w3, w2, tm, fc):
    n_slots = xs.shape[0]
    n_chunks = w1.shape[-1] // fc

    def chunk(i, c, nu):
        return jnp.where(i < nu[0], c, n_chunks - 1)

    grid_spec = pltpu.PrefetchScalarGridSpec(
        num_scalar_prefetch=2,
        grid=(n_slots // tm, n_chunks),
        in_specs=[
            pl.BlockSpec((tm, D_MODEL), lambda i, c, te, nu: (i, 0)),
            pl.BlockSpec((None, D_MODEL, fc), lambda i, c, te, nu: (te[i], 0, chunk(i, c, nu))),
            pl.BlockSpec((None, D_MODEL, fc), lambda i, c, te, nu: (te[i], 0, chunk(i, c, nu))),
            pl.BlockSpec((None, fc, D_MODEL), lambda i, c, te, nu: (te[i], chunk(i, c, nu), 0)),
        ],
        out_specs=pl.BlockSpec((tm, D_MODEL), lambda i, c, te, nu: (i, 0)),
        scratch_shapes=[pltpu.VMEM((tm, D_MODEL), BF16)],
    )
    return pl.pallas_call(
        _expert_kernel,
        grid_spec=grid_spec,
        out_shape=jax.ShapeDtypeStruct((n_slots, D_MODEL), F32),
        compiler_params=_params("arbitrary", "arbitrary"),
        name="moe_experts",
    )(tile_expert, n_used, xs, w1, w3, w2)


def _combine_kernel(pos_ref, x_ref, wcol_ref, ys_ref, o_ref, buf, sem):
    tm = x_ref.shape[0]

    def issue(t, carry):
        for k in range(2):
            _row_copy(ys_ref, pos_ref[k, t], buf.at[k], t, sem).start()
        return carry

    lax.fori_loop(0, tm, issue, 0, unroll=8)
    for k in range(2):
        pltpu.make_async_copy(ys_ref.at[pl.ds(0, tm)], buf.at[k], sem).wait()
    w = wcol_ref[...]
    o_ref[...] = x_ref[...] + w[:, 0:1] * buf[0] + w[:, 1:2] * buf[1]


def _combine(pos, x, wcol, ys, tm):
    s = x.shape[0]
    return pl.pallas_call(
        _combine_kernel,
        grid=(s // tm,),
        in_specs=[pl.BlockSpec((2, tm), lambda i: (0, i), memory_space=pltpu.SMEM),
                  _row_spec(tm, D_MODEL), _row_spec(tm, LANE), pl.BlockSpec(memory_space=pl.ANY)],
        out_specs=_row_spec(tm, D_MODEL),
        out_shape=jax.ShapeDtypeStruct((s, D_MODEL), F32),
        scratch_shapes=[pltpu.VMEM((2, tm, D_MODEL), F32), pltpu.SemaphoreType.DMA(())],
        compiler_params=_params("arbitrary"),
        name="moe_combine",
    )(pos, x, wcol, ys)


def _moe(x, g, w_router, w1, w3, w2, tm, fc):
    s = x.shape[0]
    n_tiles = 2 * s // tm + N_EXPERTS
    wrt = jnp.pad(w_router.T, ((0, 16 - N_EXPERTS), (0, 0))).astype(BF16)
    h, e, rank, wcol, cnt = _route(x, g, wrt, tm)
    counts = cnt[:, 0].astype(jnp.int32)
    padded = (counts + tm - 1) // tm * tm
    ends = jnp.cumsum(padded)
    offs = ends - padded
    pos = rank
    for ex in range(N_EXPERTS):
        pos = pos + jnp.where(e == ex, offs[ex], 0)
    tile_start = jnp.arange(n_tiles, dtype=jnp.int32) * tm
    tile_expert = jnp.minimum(jnp.sum(tile_start[:, None] >= ends[None, :], axis=1),
                              N_EXPERTS - 1).astype(jnp.int32)
    n_used = (ends[-1:] // tm).astype(jnp.int32)
    xs = _dispatch(pos, h, n_tiles * tm, tm)
    ys = _experts(tile_expert, n_used, xs, w1, w3, w2, tm, fc)
    return _combine(pos, x, wcol, ys, tm)


def _final_norm_kernel(x_ref, g_ref, o_ref):
    o_ref[...] = _rms(x_ref[...], g_ref[...])


def _final_norm(x, g, tm):
    s = x.shape[0]
    return pl.pallas_call(
        _final_norm_kernel,
        grid=(s // tm,),
        in_specs=[_row_spec(tm, D_MODEL), _const_spec(g.shape)],
        out_specs=_row_spec(tm, D_MODEL),
        out_shape=jax.ShapeDtypeStruct((s, D_MODEL), F32),
        compiler_params=_params("arbitrary"),
        name="final_norm",
    )(x, g)


def _pad_heads(w, width, padded=HEAD_PAD):
    r = w.shape[0]
    w = w.reshape(r, MLA_HEADS, width)
    return jnp.pad(w, ((0, 0), (0, 0), (0, padded - width))).reshape(r, MLA_HEADS * padded)


def _rotate_half_cols(w_rope):
    half = MLA_ROPE // 2
    return jnp.concatenate([-w_rope[..., half:], w_rope[..., :half]], axis=-1)


_W_IN_SEGMENTS = (("w_dq", Q_RANK), ("w_dkv", KV_RANK), ("w_kr", MLA_ROPE), ("w_sg", 2 * SG_WIDTH),
                  ("w_xq", X_HEADS * X_HEAD_DIM), ("w_gate", N_BRANCH * D_MODEL))
_W_IN_SPLIT = tuple(name for name, _ in _W_IN_SEGMENTS if name != "w_kr")


def _split_w_in_kernel(w_ref, *outs):
    start = 0
    out = iter(outs)
    for name, width in _W_IN_SEGMENTS:
        if name in _W_IN_SPLIT:
            next(out)[...] = w_ref[:, start:start + width].astype(BF16)
        start += width


def _split_w_in(w_in):
    depth, d, total = w_in.shape
    rows = 256
    widths = dict(_W_IN_SEGMENTS)
    outs = pl.pallas_call(
        _split_w_in_kernel,
        grid=(depth, d // rows),
        in_specs=[pl.BlockSpec((None, rows, total), lambda l, i: (l, i, 0))],
        out_specs=[pl.BlockSpec((None, rows, widths[n]), lambda l, i: (l, i, 0)) for n in _W_IN_SPLIT],
        out_shape=[jax.ShapeDtypeStruct((depth, d, widths[n]), BF16) for n in _W_IN_SPLIT],
        compiler_params=_params("arbitrary", "arbitrary"),
        name="split_w_in",
    )(w_in)
    return dict(zip(_W_IN_SPLIT, outs))


def _layer_weights(l, w_split, g_mix, w_in, g_q, g_kv, w_uq, w_uk, w_uv, g_sg, w_s, b_s, w_br,
                   b_gate, w_out):
    o_kr = Q_RANK + KV_RANK
    o_sg = o_kr + MLA_ROPE
    w_kr = w_in[l, :, o_kr:o_sg]
    rope_pad = ((0, 0), (MLA_NOPE, HEAD_PAD - MLA_QK))
    w_kr2 = jnp.concatenate([jnp.pad(w_kr, rope_pad), jnp.pad(_rotate_half_cols(w_kr), rope_pad)],
                            axis=1)
    uq = w_uq[l].reshape(Q_RANK, MLA_HEADS, MLA_QK)
    uq_rot = _rotate_half_cols(uq[..., MLA_NOPE:]).reshape(Q_RANK, MLA_HEADS * MLA_ROPE)
    w_br0 = jnp.pad(w_br[l, 0].reshape(MLA_HEADS, MLA_V, D_MODEL),
                    ((0, 0), (0, HEAD_PAD - MLA_V), (0, 0))).reshape(QK_PAD, D_MODEL)
    return {
        "g_mix": g_mix[l][None, :],
        **{name: w_split[name][l] for name in _W_IN_SPLIT},
        "w_kr": w_kr2.astype(BF16),
        "g_q": g_q[l][None, :],
        "g_kv": g_kv[l][None, :],
        "w_uq_t": _pad_heads(w_uq[l], MLA_QK).T.astype(BF16),
        "w_uqr_t": uq_rot.T.astype(BF16),
        "w_uk": _pad_heads(w_uk[l], MLA_NOPE).astype(BF16),
        "w_uv_t": _pad_heads(w_uv[l], MLA_V, V_ROWS).T.astype(BF16),
        "g_sg": g_sg[l][None, :],
        "w_s": w_s[l].astype(BF16),
        "b_st": b_s[l].T,
        "b_gate": b_gate[l],
        "w_br0": w_br0.astype(BF16),
        "w_br1": w_br[l, 1].astype(BF16),
        "w_br2": w_br[l, 2].astype(BF16),
        "w_out": w_out[l].astype(BF16),
    }


def _rope_constants():
    half = MLA_ROPE // 2
    inv_freq = 1.0 / (ROPE_THETA ** (jnp.arange(0, MLA_ROPE, 2, dtype=F32) / MLA_ROPE))
    lane_freq = jnp.zeros((HEAD_PAD,), F32)
    lane_freq = lane_freq.at[MLA_NOPE:MLA_NOPE + half].set(inv_freq)
    lane_freq = lane_freq.at[MLA_NOPE + half:MLA_QK].set(inv_freq)
    v_one = jnp.zeros((MLA_HEADS, V_ROWS), F32).at[:, MLA_V].set(1.0).reshape(1, MLA_HEADS * V_ROWS)
    return lane_freq[None, :], v_one


def _tiles(s):
    tm = min(512, s)
    tq = min(512, s)
    tk = min(512, s)
    tmoe = min(512, s)
    return tm, tq, tk, tmoe


def kernel(x, mem, positions, g_mix, w_in, g_q, g_kv, w_uq, w_uk, w_uv, g_sg, w_s, b_s, g_mem, w_mkv, w_br, b_gate, w_out, g_ffn, dense_w1, dense_w3, dense_w2, w_router, moe_w1, moe_w3, moe_w2, g_final):
    b, s, d = x.shape
    assert b == 1 and d == D_MODEL and s % SG_CHUNK == 0
    depth = g_mix.shape[0]
    tm, tq, tk, tmoe = _tiles(s)
    xs = x[0]
    pos = positions[0].astype(F32)[:, None]
    inv_freq, v_one = _rope_constants()
    rope = _rope_tables(pos, inv_freq, tm)
    w_split = _split_w_in(w_in)
    for l in range(depth):
        lw = _layer_weights(l, w_split, g_mix, w_in, g_q, g_kv, w_uq, w_uk, w_uv, g_sg, w_s, b_s,
                            w_br, b_gate, w_out)
        kv_mem = _mem_kv(mem[0], g_mem[None, :], w_mkv[l].astype(BF16))
        lw["k_mem_t"] = kv_mem[:, :BRANCH_WIDTH].T
        lw["v_mem"] = kv_mem[:, BRANCH_WIDTH:]
        lw["v_one"] = v_one
        qt, k, vt, yb, yc = _mixer_pre(xs, rope, lw, tm)
        ya = _flash(qt, k, vt, tq, tk)
        xs = _mixer_post(xs, ya, yb, yc, lw, tm)
        j = l // 2
        if l % 2 == 0:
            xs = _dense_ffn(xs, g_ffn[l][None, :], dense_w1[j].astype(BF16),
                            dense_w3[j].astype(BF16), dense_w2[j].astype(BF16), tm)
        else:
            fc = moe_w1.shape[-1] // 2
            xs = _moe(xs, g_ffn[l][None, :], w_router[j], moe_w1[j].astype(BF16),
                      moe_w3[j].astype(BF16), moe_w2[j].astype(BF16), tmoe, fc)
    return _final_norm(xs, g_final[None, :], tm)[None]
```

```python
import functools

import jax
import jax.numpy as jnp
from jax import lax
from jax.experimental import pallas as pl
from jax.experimental.pallas import tpu as pltpu

D_MODEL = 1024
N_BRANCH = 3
BRANCH_WIDTH = 512
MLA_HEADS = 8
MLA_NOPE = 64
MLA_ROPE = 32
MLA_QK = MLA_NOPE + MLA_ROPE
MLA_V = BRANCH_WIDTH // MLA_HEADS
Q_RANK = 384
KV_RANK = 256
ROPE_THETA = 10000.0
SG_CHUNK = 128
SG_GROUPS = 4
SG_WIDTH = BRANCH_WIDTH
X_HEADS = 4
X_HEAD_DIM = BRANCH_WIDTH // X_HEADS
N_EXPERTS = 8
EPS = 1e-6
LOG2_E = 1.4426950408889634

LANE = 128
HEAD_PAD = LANE
QK_PAD = MLA_HEADS * HEAD_PAD
V_ROWS = 80
VMEM_LIMIT = 56 * 1024 * 1024
FLASH_QBLOCKS = 4
FLASH_UNROLL = 14

F32 = jnp.float32
BF16 = jnp.bfloat16


def _dot(a, b):
    return jnp.dot(a, b, preferred_element_type=F32)


def _dot_nt(a, b):
    return lax.dot_general(a, b, (((1,), (1,)), ((), ())), preferred_element_type=F32)


def _rms(x, g):
    return x * lax.rsqrt(jnp.mean(x * x, axis=-1, keepdims=True) + EPS) * g


def _const_spec(shape):
    nd = len(shape)
    return pl.BlockSpec(shape, lambda *_: (0,) * nd, pipeline_mode=pl.Buffered(1))


def _row_spec(tm, width):
    return pl.BlockSpec((tm, width), lambda i: (i, 0))


def _params(*sem):
    return pltpu.CompilerParams(dimension_semantics=sem, vmem_limit_bytes=VMEM_LIMIT)


def _mem_kv_kernel(mem_ref, g_ref, w_ref, o_ref):
    mem_n = _rms(mem_ref[...], g_ref[...]).astype(BF16)
    o_ref[...] = _dot(mem_n, w_ref[...]).astype(BF16)


def _mem_kv(mem, g_mem, w_mkv):
    m = mem.shape[0]
    return pl.pallas_call(
        _mem_kv_kernel,
        out_shape=jax.ShapeDtypeStruct((m, 2 * BRANCH_WIDTH), BF16),
        name="mem_kv",
    )(mem, g_mem, w_mkv)


def _rope_tables_kernel(pos_ref, posr_ref, invf_ref, invfc_ref, cs_out, sn_out, cst_out, snt_out):
    ang = pos_ref[...] * invf_ref[...]
    cs_out[...] = jnp.cos(ang)
    sn_out[...] = jnp.sin(ang)
    ang_t = invfc_ref[...] * posr_ref[...]
    cst_out[...] = jnp.cos(ang_t)
    snt_out[...] = jnp.sin(ang_t)


def _rope_tables(pos, inv_freq, tm):
    s = pos.shape[0]
    row = jax.ShapeDtypeStruct((s, HEAD_PAD), F32)
    col = jax.ShapeDtypeStruct((HEAD_PAD, s), F32)
    return pl.pallas_call(
        _rope_tables_kernel,
        grid=(s // tm,),
        in_specs=[_row_spec(tm, 1), pl.BlockSpec((1, tm), lambda i: (0, i)),
                  _const_spec(inv_freq.shape), _const_spec(inv_freq.T.shape)],
        out_specs=[_row_spec(tm, HEAD_PAD)] * 2 + [pl.BlockSpec((HEAD_PAD, tm), lambda i: (0, i))] * 2,
        out_shape=[row, row, col, col],
        compiler_params=_params("arbitrary"),
        name="rope_tables",
    )(pos, pos.reshape(1, s), inv_freq, inv_freq.T)


def _mixer_pre_kernel(x_ref, cs_ref, sn_ref, cst_ref, snt_ref, gmix_ref, wdq_ref, wdkv_ref,
                      wkr_ref, wsg_ref, wxq_ref, gq_ref, gkv_ref, wuqt_ref, wuqrt_ref, wuk_ref,
                      wuvt_ref, vonec_ref, gsg_ref, ws_ref, bst_ref, kmt_ref, vm_ref,
                      qt_out, k_out, vt_out, yb_out, yc_out):
    tm = x_ref.shape[0]
    h = _rms(x_ref[...], gmix_ref[...]).astype(BF16)

    cq = _rms(_dot(h, wdq_ref[...]), gq_ref[...]).astype(BF16)
    qa_t = _dot_nt(wuqt_ref[...], cq)
    qb_t = _dot_nt(wuqrt_ref[...], cq)
    ckv = _rms(_dot(h, wdkv_ref[...]), gkv_ref[...]).astype(BF16)
    kn = _dot(ckv, wuk_ref[...])
    kr = _dot(h, wkr_ref[...])
    k_rope = kr[:, :LANE] * cs_ref[...] + kr[:, LANE:] * sn_ref[...]
    scale = MLA_QK ** -0.5 * LOG2_E
    cs_r = cst_ref[MLA_NOPE:MLA_QK, :]
    sn_r = snt_ref[MLA_NOPE:MLA_QK, :]
    for hd in range(MLA_HEADS):
        sl = slice(hd * HEAD_PAD, (hd + 1) * HEAD_PAD)
        qa = qa_t[sl, :]
        rope = qa[MLA_NOPE:MLA_QK, :] * cs_r + qb_t[hd * MLA_ROPE:(hd + 1) * MLA_ROPE, :] * sn_r
        q_head = jnp.concatenate([qa[:MLA_NOPE, :], rope, qa[MLA_QK:, :]], axis=0)
        qt_out[sl, :] = (q_head * scale).astype(BF16)
        k_out[:, sl] = (kn[:, sl] + k_rope).astype(BF16)
    vt_out[...] = (_dot_nt(wuvt_ref[...], ckv) + vonec_ref[...]).astype(BF16)

    z = jax.nn.gelu(_dot(h, wsg_ref[...]))
    u = z[:, :SG_WIDTH]
    vn = _rms(z[:, SG_WIDTH:], gsg_ref[...]).astype(BF16)
    gw = SG_WIDTH // SG_GROUPS
    for n in range(tm // SG_CHUNK):
        rows = slice(n * SG_CHUNK, (n + 1) * SG_CHUNK)
        for g in range(SG_GROUPS):
            cols = slice(g * gw, (g + 1) * gw)
            sg = _dot(ws_ref[g], vn[rows, cols]) + bst_ref[:, g:g + 1]
            yb_out[rows, cols] = (u[rows, cols] * sg).astype(BF16)

    qx = _dot(h, wxq_ref[...]) * (X_HEAD_DIM ** -0.5)
    for hd in range(X_HEADS):
        sl = slice(hd * X_HEAD_DIM, (hd + 1) * X_HEAD_DIM)
        sc = _dot(qx[:, sl].astype(BF16), kmt_ref[sl, :])
        p = jnp.exp(sc - jnp.max(sc, axis=-1, keepdims=True))
        l = jnp.sum(p, axis=-1, keepdims=True)
        o = _dot(p.astype(BF16), vm_ref[:, sl])
        yc_out[:, sl] = (o / l).astype(BF16)


def _col_spec(height, tm):
    return pl.BlockSpec((height, tm), lambda i: (0, i))


def _mixer_pre(x, rope, lw, tm):
    s = x.shape[0]
    ins = [x, *rope, lw["g_mix"], lw["w_dq"], lw["w_dkv"], lw["w_kr"], lw["w_sg"],
           lw["w_xq"], lw["g_q"], lw["g_kv"], lw["w_uq_t"], lw["w_uqr_t"], lw["w_uk"],
           lw["w_uv_t"], lw["v_one"].T, lw["g_sg"], lw["w_s"],
           lw["b_st"], lw["k_mem_t"], lw["v_mem"]]
    in_specs = [_row_spec(tm, D_MODEL), _row_spec(tm, HEAD_PAD), _row_spec(tm, HEAD_PAD),
                _col_spec(HEAD_PAD, tm), _col_spec(HEAD_PAD, tm)] + \
               [_const_spec(a.shape) for a in ins[5:]]
    v_rows = MLA_HEADS * V_ROWS
    out_shape = [jax.ShapeDtypeStruct((QK_PAD, s), BF16), jax.ShapeDtypeStruct((s, QK_PAD), BF16),
                 jax.ShapeDtypeStruct((v_rows, s), BF16)] + \
                [jax.ShapeDtypeStruct((s, BRANCH_WIDTH), BF16)] * 2
    out_specs = [_col_spec(QK_PAD, tm), _row_spec(tm, QK_PAD), _col_spec(v_rows, tm)] + \
                [_row_spec(tm, BRANCH_WIDTH)] * 2
    return pl.pallas_call(
        _mixer_pre_kernel,
        grid=(s // tm,),
        in_specs=in_specs,
        out_specs=out_specs,
        out_shape=out_shape,
        compiler_params=_params("arbitrary"),
        name="mixer_pre",
    )(*ins)


def _flash_kernel(qt_ref, k_ref, vt_ref, o_ref, s_scr, m_scr, acc_scr, *, tq, tk):
    n = k_ref.shape[0] // tk
    groups = qt_ref.shape[1] // tq

    def scores(g, t, slot):
        start = pl.multiple_of(t * tk, tk)
        s_scr[slot] = _dot(k_ref[pl.ds(start, tk), :], qt_ref[:, g * tq:(g + 1) * tq])

    def update(g, t, slot):
        start = pl.multiple_of(t * tk, tk)
        sc = s_scr[slot]
        m = m_scr[g]
        m_new = jnp.maximum(m, jnp.max(sc, axis=0, keepdims=True))
        alpha = jnp.exp2(m - m_new)
        p = jnp.exp2(sc - m_new).astype(BF16)
        acc_scr[g] = acc_scr[g] * alpha + _dot(vt_ref[:, pl.ds(start, tk)], p)
        m_scr[g] = m_new

    def step(g, t, slot):
        if t + 1 < n:
            scores(g, t + 1, 1 - slot)
        elif g + 1 < groups:
            scores(g + 1, 0, 1 - slot)
        update(g, t, slot)

    def finish(g):
        acc = acc_scr[g]
        out_t = acc / acc[MLA_V:MLA_V + 1, :]
        pad = jnp.zeros((HEAD_PAD - V_ROWS, tq), F32)
        o_ref[g * tq:(g + 1) * tq, :] = jnp.concatenate([out_t, pad], axis=0).T.astype(BF16)

    m_scr[...] = jnp.full(m_scr.shape, -jnp.inf, F32)
    acc_scr[...] = jnp.zeros(acc_scr.shape, F32)
    scores(0, 0, 0)
    loops = max(n - 2, 0) // FLASH_UNROLL
    for g in range(groups):
        off = g * n
        step(g, 0, off % 2)
        if g > 0:
            finish(g - 1)

        def body(u, carry, g=g, off=off):
            base = 1 + u * FLASH_UNROLL
            for r in range(FLASH_UNROLL):
                scores(g, base + r + 1, (off + r) % 2)
                update(g, base + r, (off + 1 + r) % 2)
            return carry

        lax.fori_loop(0, loops, body, 0)
        for t in range(1 + loops * FLASH_UNROLL, n):
            step(g, t, (off + t) % 2)
    finish(groups - 1)


def _flash(qt, k, vt, tq, tk):
    s = k.shape[0]
    groups = FLASH_QBLOCKS if s % (FLASH_QBLOCKS * tq) == 0 else 1
    tqg = groups * tq
    return pl.pallas_call(
        functools.partial(_flash_kernel, tq=tq, tk=tk),
        grid=(MLA_HEADS, s // tqg),
        in_specs=[pl.BlockSpec((HEAD_PAD, tqg), lambda h, i: (h, i)),
                  pl.BlockSpec((s, HEAD_PAD), lambda h, i: (0, h)),
                  pl.BlockSpec((V_ROWS, s), lambda h, i: (h, 0))],
        out_specs=pl.BlockSpec((tqg, HEAD_PAD), lambda h, i: (i, h)),
        out_shape=jax.ShapeDtypeStruct((s, QK_PAD), BF16),
        scratch_shapes=[pltpu.VMEM((2, tk, tq), F32), pltpu.VMEM((groups, 1, tq), F32),
                        pltpu.VMEM((groups, V_ROWS, tq), F32)],
        compiler_params=_params("arbitrary", "arbitrary"),
        name="flash",
    )(qt, k, vt)


def _mixer_post_kernel(x_ref, ya_ref, yb_ref, yc_ref, gmix_ref, wgate_ref, bgate_ref,
                       wbr0_ref, wbr1_ref, wbr2_ref, wout_ref, o_ref):
    x = x_ref[...]
    h = _rms(x, gmix_ref[...]).astype(BF16)
    mix = None
    for n, (y_ref, w_ref) in enumerate(((ya_ref, wbr0_ref), (yb_ref, wbr1_ref), (yc_ref, wbr2_ref))):
        cols = slice(n * D_MODEL, (n + 1) * D_MODEL)
        gate = jax.nn.sigmoid(_dot(h, wgate_ref[:, cols]) + bgate_ref[n:n + 1, :])
        term = gate * _dot(y_ref[...], w_ref[...])
        mix = term if mix is None else mix + term
    o_ref[...] = x + _dot(mix.astype(BF16), wout_ref[...])


def _mixer_post(x, ya, yb, yc, lw, tm):
    s = x.shape[0]
    consts = [lw["g_mix"], lw["w_gate"], lw["b_gate"], lw["w_br0"], lw["w_br1"], lw["w_br2"],
              lw["w_out"]]
    in_specs = [_row_spec(tm, D_MODEL), _row_spec(tm, QK_PAD), _row_spec(tm, BRANCH_WIDTH),
                _row_spec(tm, BRANCH_WIDTH)] + [_const_spec(a.shape) for a in consts]
    return pl.pallas_call(
        _mixer_post_kernel,
        grid=(s // tm,),
        in_specs=in_specs,
        out_specs=_row_spec(tm, D_MODEL),
        out_shape=jax.ShapeDtypeStruct((s, D_MODEL), F32),
        compiler_params=_params("arbitrary"),
        name="mixer_post",
    )(x, ya, yb, yc, *consts)


def _dense_ffn_kernel(x_ref, g_ref, w1_ref, w3_ref, w2_ref, o_ref):
    x = x_ref[...]
    h = _rms(x, g_ref[...]).astype(BF16)
    t = jax.nn.silu(_dot(h, w1_ref[...])) * _dot(h, w3_ref[...])
    o_ref[...] = x + _dot(t.astype(BF16), w2_ref[...])


def _dense_ffn(x, g, w1, w3, w2, tm):
    s = x.shape[0]
    consts = [g, w1, w3, w2]
    return pl.pallas_call(
        _dense_ffn_kernel,
        grid=(s // tm,),
        in_specs=[_row_spec(tm, D_MODEL)] + [_const_spec(a.shape) for a in consts],
        out_specs=_row_spec(tm, D_MODEL),
        out_shape=jax.ShapeDtypeStruct((s, D_MODEL), F32),
        compiler_params=_params("arbitrary"),
        name="dense_ffn",
    )(x, *consts)


def _route_kernel(x_ref, g_ref, wrt_ref, tri_ref, h_out, e_out, rank_out, wcol_out, cnt_out,
                  carry_scr):
    tm = x_ref.shape[0]

    @pl.when(pl.program_id(0) == 0)
    def _init():
        carry_scr[...] = jnp.zeros(carry_scr.shape, F32)

    h = _rms(x_ref[...], g_ref[...])
    h_out[...] = h
    logits = _dot_nt(wrt_ref[...], h.astype(BF16))[:N_EXPERTS, :]
    row = lax.broadcasted_iota(jnp.int32, logits.shape, 0)
    m1 = jnp.max(logits, axis=0, keepdims=True)
    i1 = jnp.min(jnp.where(logits == m1, row, N_EXPERTS), axis=0, keepdims=True)
    rest = jnp.where(row == i1, -jnp.inf, logits)
    m2 = jnp.max(rest, axis=0, keepdims=True)
    i2 = jnp.min(jnp.where(rest == m2, row, N_EXPERTS), axis=0, keepdims=True)
    e2 = jnp.exp(m2 - m1)
    den = 1.0 + e2
    oh1 = (row == i1).astype(F32)
    oh2 = (row == i2).astype(F32)
    cum = _dot(jnp.concatenate([oh1, oh2], axis=0).astype(BF16), tri_ref[...])
    tot = cum[:N_EXPERTS] + cum[N_EXPERTS:] + carry_scr[:, 0:1]
    rank1 = jnp.sum(oh1 * tot, axis=0, keepdims=True) - 1.0
    rank2 = jnp.sum(oh2 * tot, axis=0, keepdims=True) - 1.0
    e_out[...] = jnp.concatenate([i1, i2], axis=0)
    rank_out[...] = jnp.concatenate([rank1, rank2], axis=0).astype(jnp.int32)
    carry_scr[...] = jnp.broadcast_to(tot[:, tm - 1:tm], carry_scr.shape)
    cnt_out[...] = carry_scr[...]
    w_rows = jnp.concatenate([1.0 / den, e2 / den, jnp.zeros((LANE - 2, tm), F32)], axis=0)
    wcol_out[...] = w_rows.T


def _route(x, g, w_router_t, tm):
    s = x.shape[0]
    tri = (jnp.arange(tm)[:, None] <= jnp.arange(tm)[None, :]).astype(BF16)
    pair = lambda dt: jax.ShapeDtypeStruct((2, s), dt)
    return pl.pallas_call(
        _route_kernel,
        grid=(s // tm,),
        in_specs=[_row_spec(tm, D_MODEL), _const_spec(g.shape), _const_spec(w_router_t.shape),
                  _const_spec(tri.shape)],
        out_specs=[_row_spec(tm, D_MODEL), _col_spec(2, tm), _col_spec(2, tm), _row_spec(tm, LANE),
                   pl.BlockSpec((N_EXPERTS, LANE), lambda i: (0, 0))],
        out_shape=[jax.ShapeDtypeStruct((s, D_MODEL), F32), pair(jnp.int32), pair(jnp.int32),
                   jax.ShapeDtypeStruct((s, LANE), F32),
                   jax.ShapeDtypeStruct((N_EXPERTS, LANE), F32)],
        scratch_shapes=[pltpu.VMEM((N_EXPERTS, LANE), F32)],
        compiler_params=_params("arbitrary"),
        name="moe_route",
    )(x, g, w_router_t, tri)


def _row_copy(src_ref, src_row, dst_ref, dst_row, sem):
    return pltpu.make_async_copy(src_ref.at[pl.ds(src_row, 1)], dst_ref.at[pl.ds(dst_row, 1)], sem)


def _dispatch_kernel(pos_ref, h_ref, xs_in_ref, xs_ref, sem):
    del xs_in_ref
    tm = h_ref.shape[0]

    def issue(t, carry):
        for k in range(2):
            _row_copy(h_ref, t, xs_ref, pos_ref[k, t], sem).start()
        return carry

    lax.fori_loop(0, tm, issue, 0, unroll=8)
    for k in range(2):
        pltpu.make_async_copy(h_ref, xs_ref.at[pl.ds(0, tm)], sem).wait()


def _dispatch(pos, h, n_slots, tm):
    s = h.shape[0]
    xs0 = jnp.zeros((n_slots, D_MODEL), F32)
    return pl.pallas_call(
        _dispatch_kernel,
        grid=(s // tm,),
        in_specs=[pl.BlockSpec((2, tm), lambda i: (0, i), memory_space=pltpu.SMEM),
                  _row_spec(tm, D_MODEL), pl.BlockSpec(memory_space=pl.ANY)],
        out_specs=pl.BlockSpec(memory_space=pl.ANY),
        out_shape=jax.ShapeDtypeStruct((n_slots, D_MODEL), F32),
        scratch_shapes=[pltpu.SemaphoreType.DMA(())],
        input_output_aliases={2: 0},
        compiler_params=pltpu.CompilerParams(dimension_semantics=("arbitrary",),
                                             vmem_limit_bytes=VMEM_LIMIT, has_side_effects=True),
        name="moe_dispatch",
    )(pos, h, xs0)


def _expert_kernel(te_ref, nu_ref, xs_ref, w1_ref, w3_ref, w2_ref, ys_ref, h_scr):
    del te_ref
    c = pl.program_id(1)

    @pl.when(c == 0)
    def _load():
        h_scr[...] = xs_ref[...].astype(BF16)
        ys_ref[...] = jnp.zeros(ys_ref.shape, F32)

    @pl.when(pl.program_id(0) < nu_ref[0])
    def _ffn():
        h = h_scr[...]
        t = jax.nn.silu(_dot(h, w1_ref[...])) * _dot(h, w3_ref[...])
        ys_ref[...] += _dot(t.astype(BF16), w2_ref[...])


def _experts(tile_expert, n_used, xs, w1, w3, w2, j, tm, fc):
    n_slots = xs.shape[0]
    n_chunks = w1.shape[-1] // fc

    def chunk(i, c, nu):
        return jnp.where(i < nu[0], c, n_chunks - 1)

    grid_spec = pltpu.PrefetchScalarGridSpec(
        num_scalar_prefetch=2,
        grid=(n_slots // tm, n_chunks),
        in_specs=[
            pl.BlockSpec((tm, D_MODEL), lambda i, c, te, nu: (i, 0)),
            pl.BlockSpec((None, None, D_MODEL, fc),
                         lambda i, c, te, nu: (j, te[i], 0, chunk(i, c, nu))),
            pl.BlockSpec((None, None, D_MODEL, fc),
                         lambda i, c, te, nu: (j, te[i], 0, chunk(i, c, nu))),
            pl.BlockSpec((None, None, fc, D_MODEL),
                         lambda i, c, te, nu: (j, te[i], chunk(i, c, nu), 0)),
        ],
        out_specs=pl.BlockSpec((tm, D_MODEL), lambda i, c, te, nu: (i, 0)),
        scratch_shapes=[pltpu.VMEM((tm, D_MODEL), BF16)],
    )
    return pl.pallas_call(
        _expert_kernel,
        grid_spec=grid_spec,
        out_shape=jax.ShapeDtypeStruct((n_slots, D_MODEL), F32),
        compiler_params=_params("arbitrary", "arbitrary"),
        name="moe_experts",
    )(tile_expert, n_used, xs, w1, w3, w2)


def _combine_kernel(pos_ref, x_ref, wcol_ref, ys_ref, o_ref, buf, sem):
    tm = x_ref.shape[0]

    def issue(t, carry):
        for k in range(2):
            _row_copy(ys_ref, pos_ref[k, t], buf.at[k], t, sem).start()
        return carry

    lax.fori_loop(0, tm, issue, 0, unroll=8)
    for k in range(2):
        pltpu.make_async_copy(ys_ref.at[pl.ds(0, tm)], buf.at[k], sem).wait()
    w = wcol_ref[...]
    o_ref[...] = x_ref[...] + w[:, 0:1] * buf[0] + w[:, 1:2] * buf[1]


def _combine(pos, x, wcol, ys, tm):
    s = x.shape[0]
    return pl.pallas_call(
        _combine_kernel,
        grid=(s // tm,),
        in_specs=[pl.BlockSpec((2, tm), lambda i: (0, i), memory_space=pltpu.SMEM),
                  _row_spec(tm, D_MODEL), _row_spec(tm, LANE), pl.BlockSpec(memory_space=pl.ANY)],
        out_specs=_row_spec(tm, D_MODEL),
        out_shape=jax.ShapeDtypeStruct((s, D_MODEL), F32),
        scratch_shapes=[pltpu.VMEM((2, tm, D_MODEL), F32), pltpu.SemaphoreType.DMA(())],
        compiler_params=_params("arbitrary"),
        name="moe_combine",
    )(pos, x, wcol, ys)


def _moe(x, g, w_router, w1, w3, w2, j, tm, fc):
    s = x.shape[0]
    n_tiles = 2 * s // tm + N_EXPERTS
    wrt = jnp.pad(w_router.T, ((0, 16 - N_EXPERTS), (0, 0))).astype(BF16)
    h, e, rank, wcol, cnt = _route(x, g, wrt, tm)
    counts = cnt[:, 0].astype(jnp.int32)
    padded = (counts + tm - 1) // tm * tm
    ends = jnp.cumsum(padded)
    offs = ends - padded
    pos = rank
    for ex in range(N_EXPERTS):
        pos = pos + jnp.where(e == ex, offs[ex], 0)
    tile_start = jnp.arange(n_tiles, dtype=jnp.int32) * tm
    tile_expert = jnp.minimum(jnp.sum(tile_start[:, None] >= ends[None, :], axis=1),
                              N_EXPERTS - 1).astype(jnp.int32)
    n_used = (ends[-1:] // tm).astype(jnp.int32)
    xs = _dispatch(pos, h, n_tiles * tm, tm)
    ys = _experts(tile_expert, n_used, xs, w1, w3, w2, j, tm, fc)
    return _combine(pos, x, wcol, ys, tm)


def _final_norm_kernel(x_ref, g_ref, o_ref):
    o_ref[...] = _rms(x_ref[...], g_ref[...])


def _final_norm(x, g, tm):
    s = x.shape[0]
    return pl.pallas_call(
        _final_norm_kernel,
        grid=(s // tm,),
        in_specs=[_row_spec(tm, D_MODEL), _const_spec(g.shape)],
        out_specs=_row_spec(tm, D_MODEL),
        out_shape=jax.ShapeDtypeStruct((s, D_MODEL), F32),
        compiler_params=_params("arbitrary"),
        name="final_norm",
    )(x, g)


def _pad_heads(w, width, padded=HEAD_PAD):
    r = w.shape[0]
    w = w.reshape(r, MLA_HEADS, width)
    return jnp.pad(w, ((0, 0), (0, 0), (0, padded - width))).reshape(r, MLA_HEADS * padded)


def _rotate_half_cols(w_rope):
    half = MLA_ROPE // 2
    return jnp.concatenate([-w_rope[..., half:], w_rope[..., :half]], axis=-1)


_W_IN_SEGMENTS = (("w_dq", Q_RANK), ("w_dkv", KV_RANK), ("w_kr", MLA_ROPE), ("w_sg", 2 * SG_WIDTH),
                  ("w_xq", X_HEADS * X_HEAD_DIM), ("w_gate", N_BRANCH * D_MODEL))
_W_IN_SPLIT = tuple(name for name, _ in _W_IN_SEGMENTS if name != "w_kr")


def _split_w_in_kernel(w_ref, *outs):
    start = 0
    out = iter(outs)
    for name, width in _W_IN_SEGMENTS:
        if name in _W_IN_SPLIT:
            next(out)[...] = w_ref[:, start:start + width].astype(BF16)
        start += width


def _split_w_in(w_in, l):
    _, d, total = w_in.shape
    rows = 256
    widths = dict(_W_IN_SEGMENTS)
    outs = pl.pallas_call(
        _split_w_in_kernel,
        grid=(d // rows,),
        in_specs=[pl.BlockSpec((None, rows, total), lambda i: (l, i, 0))],
        out_specs=[_row_spec(rows, widths[n]) for n in _W_IN_SPLIT],
        out_shape=[jax.ShapeDtypeStruct((d, widths[n]), BF16) for n in _W_IN_SPLIT],
        compiler_params=_params("arbitrary"),
        name="split_w_in",
    )(w_in)
    return dict(zip(_W_IN_SPLIT, outs))


def _layer_weights(l, g_mix, w_in, g_q, g_kv, w_uq, w_uk, w_uv, g_sg, w_s, b_s, w_br,
                   b_gate, w_out):
    o_kr = Q_RANK + KV_RANK
    o_sg = o_kr + MLA_ROPE
    w_kr = w_in[l, :, o_kr:o_sg]
    rope_pad = ((0, 0), (MLA_NOPE, HEAD_PAD - MLA_QK))
    w_kr2 = jnp.concatenate([jnp.pad(w_kr, rope_pad), jnp.pad(_rotate_half_cols(w_kr), rope_pad)],
                            axis=1)
    uq = w_uq[l].reshape(Q_RANK, MLA_HEADS, MLA_QK)
    uq_rot = _rotate_half_cols(uq[..., MLA_NOPE:]).reshape(Q_RANK, MLA_HEADS * MLA_ROPE)
    w_br0 = jnp.pad(w_br[l, 0].reshape(MLA_HEADS, MLA_V, D_MODEL),
                    ((0, 0), (0, HEAD_PAD - MLA_V), (0, 0))).reshape(QK_PAD, D_MODEL)
    return {
        "g_mix": g_mix[l][None, :],
        **_split_w_in(w_in, l),
        "w_kr": w_kr2.astype(BF16),
        "g_q": g_q[l][None, :],
        "g_kv": g_kv[l][None, :],
        "w_uq_t": _pad_heads(w_uq[l], MLA_QK).T.astype(BF16),
        "w_uqr_t": uq_rot.T.astype(BF16),
        "w_uk": _pad_heads(w_uk[l], MLA_NOPE).astype(BF16),
        "w_uv_t": _pad_heads(w_uv[l], MLA_V, V_ROWS).T.astype(BF16),
        "g_sg": g_sg[l][None, :],
        "w_s": w_s[l].astype(BF16),
        "b_st": b_s[l].T,
        "b_gate": b_gate[l],
        "w_br0": w_br0.astype(BF16),
        "w_br1": w_br[l, 1].astype(BF16),
        "w_br2": w_br[l, 2].astype(BF16),
        "w_out": w_out[l].astype(BF16),
    }


def _rope_constants():
    half = MLA_ROPE // 2
    inv_freq = 1.0 / (ROPE_THETA ** (jnp.arange(0, MLA_ROPE, 2, dtype=F32) / MLA_ROPE))
    lane_freq = jnp.zeros((HEAD_PAD,), F32)
    lane_freq = lane_freq.at[MLA_NOPE:MLA_NOPE + half].set(inv_freq)
    lane_freq = lane_freq.at[MLA_NOPE + half:MLA_QK].set(inv_freq)
    v_one = jnp.zeros((MLA_HEADS, V_ROWS), F32).at[:, MLA_V].set(1.0).reshape(1, MLA_HEADS * V_ROWS)
    return lane_freq[None, :], v_one


def _tiles(s):
    tm = min(512, s)
    tq = min(512, s)
    tk = min(512, s)
    tmoe = min(512, s)
    return tm, tq, tk, tmoe


def kernel(x, mem, positions, g_mix, w_in, g_q, g_kv, w_uq, w_uk, w_uv, g_sg, w_s, b_s, g_mem, w_mkv, w_br, b_gate, w_out, g_ffn, dense_w1, dense_w3, dense_w2, w_router, moe_w1, moe_w3, moe_w2, g_final):
    b, s, d = x.shape
    assert b == 1 and d == D_MODEL and s % SG_CHUNK == 0
    depth = g_mix.shape[0]
    tm, tq, tk, tmoe = _tiles(s)
    xs = x[0]
    pos = positions[0].astype(F32)[:, None]
    inv_freq, v_one = _rope_constants()
    rope = _rope_tables(pos, inv_freq, tm)
    moe_bf16 = [w.astype(BF16) for w in (moe_w1, moe_w3, moe_w2)]
    for l in range(depth):
        lw = _layer_weights(l, g_mix, w_in, g_q, g_kv, w_uq, w_uk, w_uv, g_sg, w_s, b_s, w_br,
                            b_gate, w_out)
        kv_mem = _mem_kv(mem[0], g_mem[None, :], w_mkv[l].astype(BF16))
        lw["k_mem_t"] = kv_mem[:, :BRANCH_WIDTH].T
        lw["v_mem"] = kv_mem[:, BRANCH_WIDTH:]
        lw["v_one"] = v_one
        qt, k, vt, yb, yc = _mixer_pre(xs, rope, lw, tm)
        ya = _flash(qt, k, vt, tq, tk)
        xs = _mixer_post(xs, ya, yb, yc, lw, tm)
        j = l // 2
        if l % 2 == 0:
            xs = _dense_ffn(xs, g_ffn[l][None, :], dense_w1[j].astype(BF16),
                            dense_w3[j].astype(BF16), dense_w2[j].astype(BF16), tm)
        else:
            fc = moe_w1.shape[-1] // 2
            xs = _moe(xs, g_ffn[l][None, :], w_router[j], *moe_bf16, j, tmoe, fc)
    return _final_norm(xs, g_final[None, :], tm)[None]
```

```python
import functools

import jax
import jax.numpy as jnp
from jax import lax
from jax.experimental import pallas as pl
from jax.experimental.pallas import tpu as pltpu

D_MODEL = 1024
N_BRANCH = 3
BRANCH_WIDTH = 512
MLA_HEADS = 8
MLA_NOPE = 64
MLA_ROPE = 32
MLA_QK = MLA_NOPE + MLA_ROPE
MLA_V = BRANCH_WIDTH // MLA_HEADS
Q_RANK = 384
KV_RANK = 256
ROPE_THETA = 10000.0
SG_CHUNK = 128
SG_GROUPS = 4
SG_WIDTH = BRANCH_WIDTH
X_HEADS = 4
X_HEAD_DIM = BRANCH_WIDTH // X_HEADS
N_EXPERTS = 8
EPS = 1e-6
LOG2_E = 1.4426950408889634

LANE = 128
HEAD_PAD = LANE
QK_PAD = MLA_HEADS * HEAD_PAD
V_ROWS = 80
VMEM_LIMIT = 56 * 1024 * 1024
FLASH_QBLOCKS = 4
FLASH_UNROLL = 14

F32 = jnp.float32
BF16 = jnp.bfloat16


def _dot(a, b):
    return jnp.dot(a, b, preferred_element_type=F32)


def _dot_nt(a, b):
    return lax.dot_general(a, b, (((1,), (1,)), ((), ())), preferred_element_type=F32)


def _rms(x, g):
    return x * lax.rsqrt(jnp.mean(x * x, axis=-1, keepdims=True) + EPS) * g


def _const_spec(shape):
    nd = len(shape)
    return pl.BlockSpec(shape, lambda *_: (0,) * nd, pipeline_mode=pl.Buffered(1))


def _row_spec(tm, width):
    return pl.BlockSpec((tm, width), lambda i: (i, 0))


def _params(*sem):
    return pltpu.CompilerParams(dimension_semantics=sem, vmem_limit_bytes=VMEM_LIMIT)


def _mem_kv_kernel(mem_ref, g_ref, w_ref, o_ref):
    mem_n = _rms(mem_ref[...], g_ref[...]).astype(BF16)
    o_ref[...] = _dot(mem_n, w_ref[...]).astype(BF16)


def _mem_kv(mem, g_mem, w_mkv):
    m = mem.shape[0]
    return pl.pallas_call(
        _mem_kv_kernel,
        out_shape=jax.ShapeDtypeStruct((m, 2 * BRANCH_WIDTH), BF16),
        name="mem_kv",
    )(mem, g_mem, w_mkv)


def _rope_tables_kernel(pos_ref, posr_ref, invf_ref, invfc_ref, cs_out, sn_out, cst_out, snt_out):
    ang = pos_ref[...] * invf_ref[...]
    cs_out[...] = jnp.cos(ang)
    sn_out[...] = jnp.sin(ang)
    ang_t = invfc_ref[...] * posr_ref[...]
    cst_out[...] = jnp.cos(ang_t)
    snt_out[...] = jnp.sin(ang_t)


def _rope_tables(pos, inv_freq, tm):
    s = pos.shape[0]
    row = jax.ShapeDtypeStruct((s, HEAD_PAD), F32)
    col = jax.ShapeDtypeStruct((HEAD_PAD, s), F32)
    return pl.pallas_call(
        _rope_tables_kernel,
        grid=(s // tm,),
        in_specs=[_row_spec(tm, 1), pl.BlockSpec((1, tm), lambda i: (0, i)),
                  _const_spec(inv_freq.shape), _const_spec(inv_freq.T.shape)],
        out_specs=[_row_spec(tm, HEAD_PAD)] * 2 + [pl.BlockSpec((HEAD_PAD, tm), lambda i: (0, i))] * 2,
        out_shape=[row, row, col, col],
        compiler_params=_params("arbitrary"),
        name="rope_tables",
    )(pos, pos.reshape(1, s), inv_freq, inv_freq.T)


def _mixer_pre_kernel(x_ref, cs_ref, sn_ref, cst_ref, snt_ref, gmix_ref, wdq_ref, wdkv_ref,
                      wkr_ref, wsg_ref, wxq_ref, gq_ref, gkv_ref, wuqt_ref, wuqrt_ref, wuk_ref,
                      wuvt_ref, vonec_ref, gsg_ref, ws_ref, bst_ref, kmt_ref, vm_ref,
                      qt_out, k_out, vt_out, yb_out, yc_out):
    tm = x_ref.shape[0]
    h = _rms(x_ref[...], gmix_ref[...]).astype(BF16)

    cq = _rms(_dot(h, wdq_ref[...]), gq_ref[...]).astype(BF16)
    qa_t = _dot_nt(wuqt_ref[...], cq)
    qb_t = _dot_nt(wuqrt_ref[...], cq)
    ckv = _rms(_dot(h, wdkv_ref[...]), gkv_ref[...]).astype(BF16)
    kn = _dot(ckv, wuk_ref[...])
    kr = _dot(h, wkr_ref[...])
    k_rope = kr[:, :LANE] * cs_ref[...] + kr[:, LANE:] * sn_ref[...]
    scale = MLA_QK ** -0.5 * LOG2_E
    cs_r = cst_ref[MLA_NOPE:MLA_QK, :]
    sn_r = snt_ref[MLA_NOPE:MLA_QK, :]
    for hd in range(MLA_HEADS):
        sl = slice(hd * HEAD_PAD, (hd + 1) * HEAD_PAD)
        qa = qa_t[sl, :]
        rope = qa[MLA_NOPE:MLA_QK, :] * cs_r + qb_t[hd * MLA_ROPE:(hd + 1) * MLA_ROPE, :] * sn_r
        q_head = jnp.concatenate([qa[:MLA_NOPE, :], rope, qa[MLA_QK:, :]], axis=0)
        qt_out[sl, :] = (q_head * scale).astype(BF16)
        k_out[:, sl] = (kn[:, sl] + k_rope).astype(BF16)
    vt_out[...] = (_dot_nt(wuvt_ref[...], ckv) + vonec_ref[...]).astype(BF16)

    z = jax.nn.gelu(_dot(h, wsg_ref[...]))
    u = z[:, :SG_WIDTH]
    vn = _rms(z[:, SG_WIDTH:], gsg_ref[...]).astype(BF16)
    gw = SG_WIDTH // SG_GROUPS
    for n in range(tm // SG_CHUNK):
        rows = slice(n * SG_CHUNK, (n + 1) * SG_CHUNK)
        for g in range(SG_GROUPS):
            cols = slice(g * gw, (g + 1) * gw)
            sg = _dot(ws_ref[g], vn[rows, cols]) + bst_ref[:, g:g + 1]
            yb_out[rows, cols] = (u[rows, cols] * sg).astype(BF16)

    qx = _dot(h, wxq_ref[...]) * (X_HEAD_DIM ** -0.5)
    for hd in range(X_HEADS):
        sl = slice(hd * X_HEAD_DIM, (hd + 1) * X_HEAD_DIM)
        sc = _dot(qx[:, sl].astype(BF16), kmt_ref[sl, :])
        p = jnp.exp(sc - jnp.max(sc, axis=-1, keepdims=True))
        l = jnp.sum(p, axis=-1, keepdims=True)
        o = _dot(p.astype(BF16), vm_ref[:, sl])
        yc_out[:, sl] = (o / l).astype(BF16)


def _col_spec(height, tm):
    return pl.BlockSpec((height, tm), lambda i: (0, i))


def _mixer_pre(x, rope, lw, tm):
    s = x.shape[0]
    ins = [x, *rope, lw["g_mix"], lw["w_dq"], lw["w_dkv"], lw["w_kr"], lw["w_sg"],
           lw["w_xq"], lw["g_q"], lw["g_kv"], lw["w_uq_t"], lw["w_uqr_t"], lw["w_uk"],
           lw["w_uv_t"], lw["v_one"].T, lw["g_sg"], lw["w_s"],
           lw["b_st"], lw["k_mem_t"], lw["v_mem"]]
    in_specs = [_row_spec(tm, D_MODEL), _row_spec(tm, HEAD_PAD), _row_spec(tm, HEAD_PAD),
                _col_spec(HEAD_PAD, tm), _col_spec(HEAD_PAD, tm)] + \
               [_const_spec(a.shape) for a in ins[5:]]
    v_rows = MLA_HEADS * V_ROWS
    out_shape = [jax.ShapeDtypeStruct((QK_PAD, s), BF16), jax.ShapeDtypeStruct((s, QK_PAD), BF16),
                 jax.ShapeDtypeStruct((v_rows, s), BF16)] + \
                [jax.ShapeDtypeStruct((s, BRANCH_WIDTH), BF16)] * 2
    out_specs = [_col_spec(QK_PAD, tm), _row_spec(tm, QK_PAD), _col_spec(v_rows, tm)] + \
                [_row_spec(tm, BRANCH_WIDTH)] * 2
    return pl.pallas_call(
        _mixer_pre_kernel,
        grid=(s // tm,),
        in_specs=in_specs,
        out_specs=out_specs,
        out_shape=out_shape,
        compiler_params=_params("arbitrary"),
        name="mixer_pre",
    )(*ins)


def _flash_kernel(qt_ref, k_ref, vt_ref, o_ref, s_scr, smax_scr, m_scr, acc_scr, *, tq, tk):
    n = k_ref.shape[0] // tk
    groups = qt_ref.shape[1] // tq

    def scores(g, t, slot):
        start = pl.multiple_of(t * tk, tk)
        sc = _dot(k_ref[pl.ds(start, tk), :], qt_ref[:, g * tq:(g + 1) * tq])
        s_scr[slot] = sc
        smax_scr[slot] = jnp.max(sc, axis=0, keepdims=True)

    def update(g, t, slot):
        start = pl.multiple_of(t * tk, tk)
        sc = s_scr[slot]
        m = m_scr[g]
        m_new = jnp.maximum(m, smax_scr[slot])
        alpha = jnp.exp2(m - m_new)
        p = jnp.exp2(sc - m_new).astype(BF16)
        acc_scr[g] = acc_scr[g] * alpha + _dot(vt_ref[:, pl.ds(start, tk)], p)
        m_scr[g] = m_new

    def step(g, t, slot):
        if t + 1 < n:
            scores(g, t + 1, 1 - slot)
        elif g + 1 < groups:
            scores(g + 1, 0, 1 - slot)
        update(g, t, slot)

    def finish(g):
        acc = acc_scr[g]
        out_t = acc / acc[MLA_V:MLA_V + 1, :]
        pad = jnp.zeros((HEAD_PAD - V_ROWS, tq), F32)
        o_ref[g * tq:(g + 1) * tq, :] = jnp.concatenate([out_t, pad], axis=0).T.astype(BF16)

    m_scr[...] = jnp.full(m_scr.shape, -jnp.inf, F32)
    acc_scr[...] = jnp.zeros(acc_scr.shape, F32)
    scores(0, 0, 0)
    loops = max(n - 2, 0) // FLASH_UNROLL
    for g in range(groups):
        off = g * n
        step(g, 0, off % 2)
        if g > 0:
            finish(g - 1)

        def body(u, carry, g=g, off=off):
            base = 1 + u * FLASH_UNROLL
            for r in range(FLASH_UNROLL):
                scores(g, base + r + 1, (off + r) % 2)
                update(g, base + r, (off + 1 + r) % 2)
            return carry

        lax.fori_loop(0, loops, body, 0)
        for t in range(1 + loops * FLASH_UNROLL, n):
            step(g, t, (off + t) % 2)
    finish(groups - 1)


def _flash(qt, k, vt, tq, tk):
    s = k.shape[0]
    groups = FLASH_QBLOCKS if s % (FLASH_QBLOCKS * tq) == 0 else 1
    tqg = groups * tq
    return pl.pallas_call(
        functools.partial(_flash_kernel, tq=tq, tk=tk),
        grid=(MLA_HEADS, s // tqg),
        in_specs=[pl.BlockSpec((HEAD_PAD, tqg), lambda h, i: (h, i)),
                  pl.BlockSpec((s, HEAD_PAD), lambda h, i: (0, h)),
                  pl.BlockSpec((V_ROWS, s), lambda h, i: (h, 0))],
        out_specs=pl.BlockSpec((tqg, HEAD_PAD), lambda h, i: (i, h)),
        out_shape=jax.ShapeDtypeStruct((s, QK_PAD), BF16),
        scratch_shapes=[pltpu.VMEM((2, tk, tq), F32), pltpu.VMEM((2, 1, tq), F32),
                        pltpu.VMEM((groups, 1, tq), F32),
                        pltpu.VMEM((groups, V_ROWS, tq), F32)],
        compiler_params=_params("arbitrary", "arbitrary"),
        name="flash",
    )(qt, k, vt)


def _mixer_post_kernel(x_ref, ya_ref, yb_ref, yc_ref, gmix_ref, wgate_ref, bgate_ref,
                       wbr0_ref, wbr1_ref, wbr2_ref, wout_ref, o_ref):
    x = x_ref[...]
    h = _rms(x, gmix_ref[...]).astype(BF16)
    mix = None
    for n, (y_ref, w_ref) in enumerate(((ya_ref, wbr0_ref), (yb_ref, wbr1_ref), (yc_ref, wbr2_ref))):
        cols = slice(n * D_MODEL, (n + 1) * D_MODEL)
        gate = jax.nn.sigmoid(_dot(h, wgate_ref[:, cols]) + bgate_ref[n:n + 1, :])
        term = gate * _dot(y_ref[...], w_ref[...])
        mix = term if mix is None else mix + term
    o_ref[...] = x + _dot(mix.astype(BF16), wout_ref[...])


def _mixer_post(x, ya, yb, yc, lw, tm):
    s = x.shape[0]
    consts = [lw["g_mix"], lw["w_gate"], lw["b_gate"], lw["w_br0"], lw["w_br1"], lw["w_br2"],
              lw["w_out"]]
    in_specs = [_row_spec(tm, D_MODEL), _row_spec(tm, QK_PAD), _row_spec(tm, BRANCH_WIDTH),
                _row_spec(tm, BRANCH_WIDTH)] + [_const_spec(a.shape) for a in consts]
    return pl.pallas_call(
        _mixer_post_kernel,
        grid=(s // tm,),
        in_specs=in_specs,
        out_specs=_row_spec(tm, D_MODEL),
        out_shape=jax.ShapeDtypeStruct((s, D_MODEL), F32),
        compiler_params=_params("arbitrary"),
        name="mixer_post",
    )(x, ya, yb, yc, *consts)


def _dense_ffn_kernel(x_ref, g_ref, w1_ref, w3_ref, w2_ref, o_ref):
    x = x_ref[...]
    h = _rms(x, g_ref[...]).astype(BF16)
    t = jax.nn.silu(_dot(h, w1_ref[...])) * _dot(h, w3_ref[...])
    o_ref[...] = x + _dot(t.astype(BF16), w2_ref[...])


def _dense_ffn(x, g, w1, w3, w2, tm):
    s = x.shape[0]
    consts = [g, w1, w3, w2]
    return pl.pallas_call(
        _dense_ffn_kernel,
        grid=(s // tm,),
        in_specs=[_row_spec(tm, D_MODEL)] + [_const_spec(a.shape) for a in consts],
        out_specs=_row_spec(tm, D_MODEL),
        out_shape=jax.ShapeDtypeStruct((s, D_MODEL), F32),
        compiler_params=_params("arbitrary"),
        name="dense_ffn",
    )(x, *consts)


def _route_kernel(x_ref, g_ref, wrt_ref, tri_ref, h_out, e_out, rank_out, wcol_out, cnt_out,
                  carry_scr):
    tm = x_ref.shape[0]

    @pl.when(pl.program_id(0) == 0)
    def _init():
        carry_scr[...] = jnp.zeros(carry_scr.shape, F32)

    h = _rms(x_ref[...], g_ref[...])
    h_out[...] = h
    logits = _dot_nt(wrt_ref[...], h.astype(BF16))[:N_EXPERTS, :]
    row = lax.broadcasted_iota(jnp.int32, logits.shape, 0)
    m1 = jnp.max(logits, axis=0, keepdims=True)
    i1 = jnp.min(jnp.where(logits == m1, row, N_EXPERTS), axis=0, keepdims=True)
    rest = jnp.where(row == i1, -jnp.inf, logits)
    m2 = jnp.max(rest, axis=0, keepdims=True)
    i2 = jnp.min(jnp.where(rest == m2, row, N_EXPERTS), axis=0, keepdims=True)
    e2 = jnp.exp(m2 - m1)
    den = 1.0 + e2
    oh1 = (row == i1).astype(F32)
    oh2 = (row == i2).astype(F32)
    cum = _dot(jnp.concatenate([oh1, oh2], axis=0).astype(BF16), tri_ref[...])
    tot = cum[:N_EXPERTS] + cum[N_EXPERTS:] + carry_scr[:, 0:1]
    rank1 = jnp.sum(oh1 * tot, axis=0, keepdims=True) - 1.0
    rank2 = jnp.sum(oh2 * tot, axis=0, keepdims=True) - 1.0
    e_out[...] = jnp.concatenate([i1, i2], axis=0)
    rank_out[...] = jnp.concatenate([rank1, rank2], axis=0).astype(jnp.int32)
    carry_scr[...] = jnp.broadcast_to(tot[:, tm - 1:tm], carry_scr.shape)
    cnt_out[...] = carry_scr[...]
    w_rows = jnp.concatenate([1.0 / den, e2 / den, jnp.zeros((LANE - 2, tm), F32)], axis=0)
    wcol_out[...] = w_rows.T


def _route(x, g, w_router_t, tm):
    s = x.shape[0]
    tri = (jnp.arange(tm)[:, None] <= jnp.arange(tm)[None, :]).astype(BF16)
    pair = lambda dt: jax.ShapeDtypeStruct((2, s), dt)
    return pl.pallas_call(
        _route_kernel,
        grid=(s // tm,),
        in_specs=[_row_spec(tm, D_MODEL), _const_spec(g.shape), _const_spec(w_router_t.shape),
                  _const_spec(tri.shape)],
        out_specs=[_row_spec(tm, D_MODEL), _col_spec(2, tm), _col_spec(2, tm), _row_spec(tm, LANE),
                   pl.BlockSpec((N_EXPERTS, LANE), lambda i: (0, 0))],
        out_shape=[jax.ShapeDtypeStruct((s, D_MODEL), F32), pair(jnp.int32), pair(jnp.int32),
                   jax.ShapeDtypeStruct((s, LANE), F32),
                   jax.ShapeDtypeStruct((N_EXPERTS, LANE), F32)],
        scratch_shapes=[pltpu.VMEM((N_EXPERTS, LANE), F32)],
        compiler_params=_params("arbitrary"),
        name="moe_route",
    )(x, g, w_router_t, tri)


def _row_copy(src_ref, src_row, dst_ref, dst_row, sem):
    return pltpu.make_async_copy(src_ref.at[pl.ds(src_row, 1)], dst_ref.at[pl.ds(dst_row, 1)], sem)


def _dispatch_kernel(pos_ref, h_ref, xs_in_ref, xs_ref, sem):
    del xs_in_ref
    tm = h_ref.shape[0]

    def issue(t, carry):
        for k in range(2):
            _row_copy(h_ref, t, xs_ref, pos_ref[k, t], sem).start()
        return carry

    lax.fori_loop(0, tm, issue, 0, unroll=8)
    for k in range(2):
        pltpu.make_async_copy(h_ref, xs_ref.at[pl.ds(0, tm)], sem).wait()


def _dispatch(pos, h, n_slots, tm):
    s = h.shape[0]
    xs0 = jnp.zeros((n_slots, D_MODEL), F32)
    return pl.pallas_call(
        _dispatch_kernel,
        grid=(s // tm,),
        in_specs=[pl.BlockSpec((2, tm), lambda i: (0, i), memory_space=pltpu.SMEM),
                  _row_spec(tm, D_MODEL), pl.BlockSpec(memory_space=pl.ANY)],
        out_specs=pl.BlockSpec(memory_space=pl.ANY),
        out_shape=jax.ShapeDtypeStruct((n_slots, D_MODEL), F32),
        scratch_shapes=[pltpu.SemaphoreType.DMA(())],
        input_output_aliases={2: 0},
        compiler_params=pltpu.CompilerParams(dimension_semantics=("arbitrary",),
                                             vmem_limit_bytes=VMEM_LIMIT, has_side_effects=True),
        name="moe_dispatch",
    )(pos, h, xs0)


def _expert_kernel(te_ref, nu_ref, xs_ref, w1_ref, w3_ref, w2_ref, ys_ref, h_scr):
    del te_ref
    c = pl.program_id(1)

    @pl.when(c == 0)
    def _load():
        h_scr[...] = xs_ref[...].astype(BF16)
        ys_ref[...] = jnp.zeros(ys_ref.shape, F32)

    @pl.when(pl.program_id(0) < nu_ref[0])
    def _ffn():
        h = h_scr[...]
        t = jax.nn.silu(_dot(h, w1_ref[...])) * _dot(h, w3_ref[...])
        ys_ref[...] += _dot(t.astype(BF16), w2_ref[...])


def _experts(tile_expert, n_used, xs, w1, w3, w2, j, tm, fc):
    n_slots = xs.shape[0]
    n_chunks = w1.shape[-1] // fc

    def chunk(i, c, nu):
        return jnp.where(i < nu[0], c, n_chunks - 1)

    grid_spec = pltpu.PrefetchScalarGridSpec(
        num_scalar_prefetch=2,
        grid=(n_slots // tm, n_chunks),
        in_specs=[
            pl.BlockSpec((tm, D_MODEL), lambda i, c, te, nu: (i, 0)),
            pl.BlockSpec((None, None, D_MODEL, fc),
                         lambda i, c, te, nu: (j, te[i], 0, chunk(i, c, nu))),
            pl.BlockSpec((None, None, D_MODEL, fc),
                         lambda i, c, te, nu: (j, te[i], 0, chunk(i, c, nu))),
            pl.BlockSpec((None, None, fc, D_MODEL),
                         lambda i, c, te, nu: (j, te[i], chunk(i, c, nu), 0)),
        ],
        out_specs=pl.BlockSpec((tm, D_MODEL), lambda i, c, te, nu: (i, 0)),
        scratch_shapes=[pltpu.VMEM((tm, D_MODEL), BF16)],
    )
    return pl.pallas_call(
        _expert_kernel,
        grid_spec=grid_spec,
        out_shape=jax.ShapeDtypeStruct((n_slots, D_MODEL), F32),
        compiler_params=_params("arbitrary", "arbitrary"),
        name="moe_experts",
    )(tile_expert, n_used, xs, w1, w3, w2)


def _combine_kernel(pos_ref, x_ref, wcol_ref, ys_ref, o_ref, buf, sem):
    tm = x_ref.shape[0]

    def issue(t, carry):
        for k in range(2):
            _row_copy(ys_ref, pos_ref[k, t], buf.at[k], t, sem).start()
        return carry

    lax.fori_loop(0, tm, issue, 0, unroll=8)
    for k in range(2):
        pltpu.make_async_copy(ys_ref.at[pl.ds(0, tm)], buf.at[k], sem).wait()
    w = wcol_ref[...]
    o_ref[...] = x_ref[...] + w[:, 0:1] * buf[0] + w[:, 1:2] * buf[1]


def _combine(pos, x, wcol, ys, tm):
    s = x.shape[0]
    return pl.pallas_call(
        _combine_kernel,
        grid=(s // tm,),
        in_specs=[pl.BlockSpec((2, tm), lambda i: (0, i), memory_space=pltpu.SMEM),
                  _row_spec(tm, D_MODEL), _row_spec(tm, LANE), pl.BlockSpec(memory_space=pl.ANY)],
        out_specs=_row_spec(tm, D_MODEL),
        out_shape=jax.ShapeDtypeStruct((s, D_MODEL), F32),
        scratch_shapes=[pltpu.VMEM((2, tm, D_MODEL), F32), pltpu.SemaphoreType.DMA(())],
        compiler_params=_params("arbitrary"),
        name="moe_combine",
    )(pos, x, wcol, ys)


def _moe(x, g, w_router, w1, w3, w2, j, tm, fc):
    s = x.shape[0]
    n_tiles = 2 * s // tm + N_EXPERTS
    wrt = jnp.pad(w_router.T, ((0, 16 - N_EXPERTS), (0, 0))).astype(BF16)
    h, e, rank, wcol, cnt = _route(x, g, wrt, tm)
    counts = cnt[:, 0].astype(jnp.int32)
    padded = (counts + tm - 1) // tm * tm
    ends = jnp.cumsum(padded)
    offs = ends - padded
    pos = rank
    for ex in range(N_EXPERTS):
        pos = pos + jnp.where(e == ex, offs[ex], 0)
    tile_start = jnp.arange(n_tiles, dtype=jnp.int32) * tm
    tile_expert = jnp.minimum(jnp.sum(tile_start[:, None] >= ends[None, :], axis=1),
                              N_EXPERTS - 1).astype(jnp.int32)
    n_used = (ends[-1:] // tm).astype(jnp.int32)
    xs = _dispatch(pos, h, n_tiles * tm, tm)
    ys = _experts(tile_expert, n_used, xs, w1, w3, w2, j, tm, fc)
    return _combine(pos, x, wcol, ys, tm)


def _final_norm_kernel(x_ref, g_ref, o_ref):
    o_ref[...] = _rms(x_ref[...], g_ref[...])


def _final_norm(x, g, tm):
    s = x.shape[0]
    return pl.pallas_call(
        _final_norm_kernel,
        grid=(s // tm,),
        in_specs=[_row_spec(tm, D_MODEL), _const_spec(g.shape)],
        out_specs=_row_spec(tm, D_MODEL),
        out_shape=jax.ShapeDtypeStruct((s, D_MODEL), F32),
        compiler_params=_params("arbitrary"),
        name="final_norm",
    )(x, g)


def _pad_heads(w, width, padded=HEAD_PAD):
    r = w.shape[0]
    w = w.reshape(r, MLA_HEADS, width)
    return jnp.pad(w, ((0, 0), (0, 0), (0, padded - width))).reshape(r, MLA_HEADS * padded)


def _rotate_half_cols(w_rope):
    half = MLA_ROPE // 2
    return jnp.concatenate([-w_rope[..., half:], w_rope[..., :half]], axis=-1)


_W_IN_SEGMENTS = (("w_dq", Q_RANK), ("w_dkv", KV_RANK), ("w_kr", MLA_ROPE), ("w_sg", 2 * SG_WIDTH),
                  ("w_xq", X_HEADS * X_HEAD_DIM), ("w_gate", N_BRANCH * D_MODEL))
_W_IN_SPLIT = tuple(name for name, _ in _W_IN_SEGMENTS if name != "w_kr")


def _split_w_in_kernel(w_ref, *outs):
    start = 0
    out = iter(outs)
    for name, width in _W_IN_SEGMENTS:
        if name in _W_IN_SPLIT:
            next(out)[...] = w_ref[:, start:start + width].astype(BF16)
        start += width


def _split_w_in(w_in, l):
    _, d, total = w_in.shape
    rows = 256
    widths = dict(_W_IN_SEGMENTS)
    outs = pl.pallas_call(
        _split_w_in_kernel,
        grid=(d // rows,),
        in_specs=[pl.BlockSpec((None, rows, total), lambda i: (l, i, 0))],
        out_specs=[_row_spec(rows, widths[n]) for n in _W_IN_SPLIT],
        out_shape=[jax.ShapeDtypeStruct((d, widths[n]), BF16) for n in _W_IN_SPLIT],
        compiler_params=_params("arbitrary"),
        name="split_w_in",
    )(w_in)
    return dict(zip(_W_IN_SPLIT, outs))


def _layer_weights(l, g_mix, w_in, g_q, g_kv, w_uq, w_uk, w_uv, g_sg, w_s, b_s, w_br,
                   b_gate, w_out):
    o_kr = Q_RANK + KV_RANK
    o_sg = o_kr + MLA_ROPE
    w_kr = w_in[l, :, o_kr:o_sg]
    rope_pad = ((0, 0), (MLA_NOPE, HEAD_PAD - MLA_QK))
    w_kr2 = jnp.concatenate([jnp.pad(w_kr, rope_pad), jnp.pad(_rotate_half_cols(w_kr), rope_pad)],
                            axis=1)
    uq = w_uq[l].reshape(Q_RANK, MLA_HEADS, MLA_QK)
    uq_rot = _rotate_half_cols(uq[..., MLA_NOPE:]).reshape(Q_RANK, MLA_HEADS * MLA_ROPE)
    w_br0 = jnp.pad(w_br[l, 0].reshape(MLA_HEADS, MLA_V, D_MODEL),
                    ((0, 0), (0, HEAD_PAD - MLA_V), (0, 0))).reshape(QK_PAD, D_MODEL)
    return {
        "g_mix": g_mix[l][None, :],
        **_split_w_in(w_in, l),
        "w_kr": w_kr2.astype(BF16),
        "g_q": g_q[l][None, :],
        "g_kv": g_kv[l][None, :],
        "w_uq_t": _pad_heads(w_uq[l], MLA_QK).T.astype(BF16),
        "w_uqr_t": uq_rot.T.astype(BF16),
        "w_uk": _pad_heads(w_uk[l], MLA_NOPE).astype(BF16),
        "w_uv_t": _pad_heads(w_uv[l], MLA_V, V_ROWS).T.astype(BF16),
        "g_sg": g_sg[l][None, :],
        "w_s": w_s[l].astype(BF16),
        "b_st": b_s[l].T,
        "b_gate": b_gate[l],
        "w_br0": w_br0.astype(BF16),
        "w_br1": w_br[l, 1].astype(BF16),
        "w_br2": w_br[l, 2].astype(BF16),
        "w_out": w_out[l].astype(BF16),
    }


def _rope_constants():
    half = MLA_ROPE // 2
    inv_freq = 1.0 / (ROPE_THETA ** (jnp.arange(0, MLA_ROPE, 2, dtype=F32) / MLA_ROPE))
    lane_freq = jnp.zeros((HEAD_PAD,), F32)
    lane_freq = lane_freq.at[MLA_NOPE:MLA_NOPE + half].set(inv_freq)
    lane_freq = lane_freq.at[MLA_NOPE + half:MLA_QK].set(inv_freq)
    v_one = jnp.zeros((MLA_HEADS, V_ROWS), F32).at[:, MLA_V].set(1.0).reshape(1, MLA_HEADS * V_ROWS)
    return lane_freq[None, :], v_one


def _tiles(s):
    tm = min(512, s)
    tq = min(512, s)
    tk = min(512, s)
    tmoe = min(512, s)
    return tm, tq, tk, tmoe


def kernel(x, mem, positions, g_mix, w_in, g_q, g_kv, w_uq, w_uk, w_uv, g_sg, w_s, b_s, g_mem, w_mkv, w_br, b_gate, w_out, g_ffn, dense_w1, dense_w3, dense_w2, w_router, moe_w1, moe_w3, moe_w2, g_final):
    b, s, d = x.shape
    assert b == 1 and d == D_MODEL and s % SG_CHUNK == 0
    depth = g_mix.shape[0]
    tm, tq, tk, tmoe = _tiles(s)
    xs = x[0]
    pos = positions[0].astype(F32)[:, None]
    inv_freq, v_one = _rope_constants()
    rope = _rope_tables(pos, inv_freq, tm)
    moe_bf16 = [w.astype(BF16) for w in (moe_w1, moe_w3, moe_w2)]
    for l in range(depth):
        lw = _layer_weights(l, g_mix, w_in, g_q, g_kv, w_uq, w_uk, w_uv, g_sg, w_s, b_s, w_br,
                            b_gate, w_out)
        kv_mem = _mem_kv(mem[0], g_mem[None, :], w_mkv[l].astype(BF16))
        lw["k_mem_t"] = kv_mem[:, :BRANCH_WIDTH].T
        lw["v_mem"] = kv_mem[:, BRANCH_WIDTH:]
        lw["v_one"] = v_one
        qt, k, vt, yb, yc = _mixer_pre(xs, rope, lw, tm)
        ya = _flash(qt, k, vt, tq, tk)
        xs = _mixer_post(xs, ya, yb, yc, lw, tm)
        j = l // 2
        if l % 2 == 0:
            xs = _dense_ffn(xs, g_ffn[l][None, :], dense_w1[j].astype(BF16),
                            dense_w3[j].astype(BF16), dense_w2[j].astype(BF16), tm)
        else:
            fc = moe_w1.shape[-1] // 2
            xs = _moe(xs, g_ffn[l][None, :], w_router[j], *moe_bf16, j, tmoe, fc)
    return _final_norm(xs, g_final[None, :], tm)[None]
```

```python
import functools

import jax
import jax.numpy as jnp
from jax import lax
from jax.experimental import pallas as pl
from jax.experimental.pallas import tpu as pltpu

D_MODEL = 1024
N_BRANCH = 3
BRANCH_WIDTH = 512
MLA_HEADS = 8
MLA_NOPE = 64
MLA_ROPE = 32
MLA_QK = MLA_NOPE + MLA_ROPE
MLA_V = BRANCH_WIDTH // MLA_HEADS
Q_RANK = 384
KV_RANK = 256
ROPE_THETA = 10000.0
SG_CHUNK = 128
SG_GROUPS = 4
SG_WIDTH = BRANCH_WIDTH
X_HEADS = 4
X_HEAD_DIM = BRANCH_WIDTH // X_HEADS
N_EXPERTS = 8
EPS = 1e-6
LOG2_E = 1.4426950408889634

LANE = 128
HEAD_PAD = LANE
QK_PAD = MLA_HEADS * HEAD_PAD
V_ROWS = 80
VMEM_LIMIT = 56 * 1024 * 1024
FLASH_QBLOCKS = 4
FLASH_UNROLL = 14

F32 = jnp.float32
BF16 = jnp.bfloat16


def _dot(a, b):
    return jnp.dot(a, b, preferred_element_type=F32)


def _dot_nt(a, b):
    return lax.dot_general(a, b, (((1,), (1,)), ((), ())), preferred_element_type=F32)


def _rms(x, g):
    return x * lax.rsqrt(jnp.mean(x * x, axis=-1, keepdims=True) + EPS) * g


def _const_spec(shape):
    nd = len(shape)
    return pl.BlockSpec(shape, lambda *_: (0,) * nd, pipeline_mode=pl.Buffered(1))


def _row_spec(tm, width):
    return pl.BlockSpec((tm, width), lambda i: (i, 0))


def _params(*sem):
    return pltpu.CompilerParams(dimension_semantics=sem, vmem_limit_bytes=VMEM_LIMIT)


def _mem_kv_kernel(mem_ref, g_ref, w_ref, o_ref):
    mem_n = _rms(mem_ref[...], g_ref[...]).astype(BF16)
    o_ref[...] = _dot(mem_n, w_ref[...]).astype(BF16)


def _mem_kv(mem, g_mem, w_mkv):
    m = mem.shape[0]
    return pl.pallas_call(
        _mem_kv_kernel,
        out_shape=jax.ShapeDtypeStruct((m, 2 * BRANCH_WIDTH), BF16),
        name="mem_kv",
    )(mem, g_mem, w_mkv)


def _rope_tables_kernel(pos_ref, posr_ref, invf_ref, invfc_ref, cs_out, sn_out, cst_out, snt_out):
    ang = pos_ref[...] * invf_ref[...]
    cs_out[...] = jnp.cos(ang)
    sn_out[...] = jnp.sin(ang)
    ang_t = invfc_ref[...] * posr_ref[...]
    cst_out[...] = jnp.cos(ang_t)
    snt_out[...] = jnp.sin(ang_t)


def _rope_tables(pos, inv_freq, tm):
    s = pos.shape[0]
    row = jax.ShapeDtypeStruct((s, HEAD_PAD), F32)
    col = jax.ShapeDtypeStruct((HEAD_PAD, s), F32)
    return pl.pallas_call(
        _rope_tables_kernel,
        grid=(s // tm,),
        in_specs=[_row_spec(tm, 1), pl.BlockSpec((1, tm), lambda i: (0, i)),
                  _const_spec(inv_freq.shape), _const_spec(inv_freq.T.shape)],
        out_specs=[_row_spec(tm, HEAD_PAD)] * 2 + [pl.BlockSpec((HEAD_PAD, tm), lambda i: (0, i))] * 2,
        out_shape=[row, row, col, col],
        compiler_params=_params("arbitrary"),
        name="rope_tables",
    )(pos, pos.reshape(1, s), inv_freq, inv_freq.T)


def _mixer_pre_kernel(x_ref, cs_ref, sn_ref, cst_ref, snt_ref, gmix_ref, wdq_ref, wdkv_ref,
                      wkr_ref, wsg_ref, wxq_ref, gq_ref, gkv_ref, wuqt_ref, wuqrt_ref, wuk_ref,
                      wuvt_ref, vonec_ref, gsg_ref, ws_ref, bst_ref, kmt_ref, vm_ref,
                      qt_out, k_out, vt_out, yb_out, yc_out):
    tm = x_ref.shape[0]
    h = _rms(x_ref[...], gmix_ref[...]).astype(BF16)

    cq = _rms(_dot(h, wdq_ref[...]), gq_ref[...]).astype(BF16)
    qa_t = _dot_nt(wuqt_ref[...], cq)
    qb_t = _dot_nt(wuqrt_ref[...], cq)
    ckv = _rms(_dot(h, wdkv_ref[...]), gkv_ref[...]).astype(BF16)
    kn = _dot(ckv, wuk_ref[...])
    kr = _dot(h, wkr_ref[...])
    k_rope = kr[:, :LANE] * cs_ref[...] + kr[:, LANE:] * sn_ref[...]
    scale = MLA_QK ** -0.5 * LOG2_E
    cs_r = cst_ref[MLA_NOPE:MLA_QK, :]
    sn_r = snt_ref[MLA_NOPE:MLA_QK, :]
    for hd in range(MLA_HEADS):
        sl = slice(hd * HEAD_PAD, (hd + 1) * HEAD_PAD)
        qa = qa_t[sl, :]
        rope = qa[MLA_NOPE:MLA_QK, :] * cs_r + qb_t[hd * MLA_ROPE:(hd + 1) * MLA_ROPE, :] * sn_r
        q_head = jnp.concatenate([qa[:MLA_NOPE, :], rope, qa[MLA_QK:, :]], axis=0)
        qt_out[sl, :] = (q_head * scale).astype(BF16)
        k_out[:, sl] = (kn[:, sl] + k_rope).astype(BF16)
    vt_out[...] = (_dot_nt(wuvt_ref[...], ckv) + vonec_ref[...]).astype(BF16)

    z = jax.nn.gelu(_dot(h, wsg_ref[...]))
    u = z[:, :SG_WIDTH]
    vn = _rms(z[:, SG_WIDTH:], gsg_ref[...]).astype(BF16)
    gw = SG_WIDTH // SG_GROUPS
    for n in range(tm // SG_CHUNK):
        rows = slice(n * SG_CHUNK, (n + 1) * SG_CHUNK)
        for g in range(SG_GROUPS):
            cols = slice(g * gw, (g + 1) * gw)
            sg = _dot(ws_ref[g], vn[rows, cols]) + bst_ref[:, g:g + 1]
            yb_out[rows, cols] = (u[rows, cols] * sg).astype(BF16)

    qx = _dot(h, wxq_ref[...]) * (X_HEAD_DIM ** -0.5)
    for hd in range(X_HEADS):
        sl = slice(hd * X_HEAD_DIM, (hd + 1) * X_HEAD_DIM)
        sc = _dot(qx[:, sl].astype(BF16), kmt_ref[sl, :])
        p = jnp.exp(sc - jnp.max(sc, axis=-1, keepdims=True))
        l = jnp.sum(p, axis=-1, keepdims=True)
        o = _dot(p.astype(BF16), vm_ref[:, sl])
        yc_out[:, sl] = (o / l).astype(BF16)


def _col_spec(height, tm):
    return pl.BlockSpec((height, tm), lambda i: (0, i))


def _mixer_pre(x, rope, lw, tm):
    s = x.shape[0]
    ins = [x, *rope, lw["g_mix"], lw["w_dq"], lw["w_dkv"], lw["w_kr"], lw["w_sg"],
           lw["w_xq"], lw["g_q"], lw["g_kv"], lw["w_uq_t"], lw["w_uqr_t"], lw["w_uk"],
           lw["w_uv_t"], lw["v_one"].T, lw["g_sg"], lw["w_s"],
           lw["b_st"], lw["k_mem_t"], lw["v_mem"]]
    in_specs = [_row_spec(tm, D_MODEL), _row_spec(tm, HEAD_PAD), _row_spec(tm, HEAD_PAD),
                _col_spec(HEAD_PAD, tm), _col_spec(HEAD_PAD, tm)] + \
               [_const_spec(a.shape) for a in ins[5:]]
    v_rows = MLA_HEADS * V_ROWS
    out_shape = [jax.ShapeDtypeStruct((QK_PAD, s), BF16), jax.ShapeDtypeStruct((s, QK_PAD), BF16),
                 jax.ShapeDtypeStruct((v_rows, s), BF16)] + \
                [jax.ShapeDtypeStruct((s, BRANCH_WIDTH), BF16)] * 2
    out_specs = [_col_spec(QK_PAD, tm), _row_spec(tm, QK_PAD), _col_spec(v_rows, tm)] + \
                [_row_spec(tm, BRANCH_WIDTH)] * 2
    return pl.pallas_call(
        _mixer_pre_kernel,
        grid=(s // tm,),
        in_specs=in_specs,
        out_specs=out_specs,
        out_shape=out_shape,
        compiler_params=_params("arbitrary"),
        name="mixer_pre",
    )(*ins)


def _flash_kernel(qt_ref, k_ref, vt_ref, o_ref, s_scr, m_scr, acc_scr, *, tq, tk):
    n = k_ref.shape[0] // tk
    groups = qt_ref.shape[1] // tq

    def scores(g, t, slot):
        start = pl.multiple_of(t * tk, tk)
        s_scr[slot] = _dot(k_ref[pl.ds(start, tk), :], qt_ref[:, g * tq:(g + 1) * tq])

    def update(g, t, slot):
        start = pl.multiple_of(t * tk, tk)
        sc = s_scr[slot]
        m = m_scr[g]
        m_new = jnp.maximum(m, jnp.max(sc, axis=0, keepdims=True))
        alpha = jnp.exp2(m - m_new)
        p = jnp.exp2(sc - m_new).astype(BF16)
        acc_scr[g] = acc_scr[g] * alpha + _dot(vt_ref[:, pl.ds(start, tk)], p)
        m_scr[g] = m_new

    def step(g, t, slot):
        if t + 1 < n:
            scores(g, t + 1, 1 - slot)
        elif g + 1 < groups:
            scores(g + 1, 0, 1 - slot)
        update(g, t, slot)

    def finish(g):
        acc = acc_scr[g]
        out_t = acc / acc[MLA_V:MLA_V + 1, :]
        pad = jnp.zeros((HEAD_PAD - V_ROWS, tq), F32)
        o_ref[g * tq:(g + 1) * tq, :] = jnp.concatenate([out_t, pad], axis=0).T.astype(BF16)

    m_scr[...] = jnp.full(m_scr.shape, -jnp.inf, F32)
    acc_scr[...] = jnp.zeros(acc_scr.shape, F32)
    scores(0, 0, 0)
    loops = max(n - 2, 0) // FLASH_UNROLL
    for g in range(groups):
        off = g * n
        step(g, 0, off % 2)
        if g > 0:
            finish(g - 1)

        def body(u, carry, g=g, off=off):
            base = 1 + u * FLASH_UNROLL
            for r in range(FLASH_UNROLL):
                scores(g, base + r + 1, (off + r) % 2)
                update(g, base + r, (off + 1 + r) % 2)
            return carry

        lax.fori_loop(0, loops, body, 0)
        for t in range(1 + loops * FLASH_UNROLL, n):
            step(g, t, (off + t) % 2)
    finish(groups - 1)


def _flash(qt, k, vt, tq, tk):
    s = k.shape[0]
    groups = FLASH_QBLOCKS if s % (FLASH_QBLOCKS * tq) == 0 else 1
    tqg = groups * tq
    return pl.pallas_call(
        functools.partial(_flash_kernel, tq=tq, tk=tk),
        grid=(MLA_HEADS, s // tqg),
        in_specs=[pl.BlockSpec((HEAD_PAD, tqg), lambda h, i: (h, i)),
                  pl.BlockSpec((s, HEAD_PAD), lambda h, i: (0, h)),
                  pl.BlockSpec((V_ROWS, s), lambda h, i: (h, 0))],
        out_specs=pl.BlockSpec((tqg, HEAD_PAD), lambda h, i: (i, h)),
        out_shape=jax.ShapeDtypeStruct((s, QK_PAD), BF16),
        scratch_shapes=[pltpu.VMEM((2, tk, tq), F32), pltpu.VMEM((groups, 1, tq), F32),
                        pltpu.VMEM((groups, V_ROWS, tq), F32)],
        compiler_params=_params("arbitrary", "arbitrary"),
        name="flash",
    )(qt, k, vt)


def _mixer_post_kernel(x_ref, ya_ref, yb_ref, yc_ref, gmix_ref, wgate_ref, bgate_ref,
                       wbr0_ref, wbr1_ref, wbr2_ref, wout_ref, o_ref):
    x = x_ref[...]
    h = _rms(x, gmix_ref[...]).astype(BF16)
    mix = None
    for n, (y_ref, w_ref) in enumerate(((ya_ref, wbr0_ref), (yb_ref, wbr1_ref), (yc_ref, wbr2_ref))):
        cols = slice(n * D_MODEL, (n + 1) * D_MODEL)
        gate = jax.nn.sigmoid(_dot(h, wgate_ref[:, cols]) + bgate_ref[n:n + 1, :])
        term = gate * _dot(y_ref[...], w_ref[...])
        mix = term if mix is None else mix + term
    o_ref[...] = x + _dot(mix.astype(BF16), wout_ref[...])


def _mixer_post(x, ya, yb, yc, lw, tm):
    s = x.shape[0]
    consts = [lw["g_mix"], lw["w_gate"], lw["b_gate"], lw["w_br0"], lw["w_br1"], lw["w_br2"],
              lw["w_out"]]
    in_specs = [_row_spec(tm, D_MODEL), _row_spec(tm, QK_PAD), _row_spec(tm, BRANCH_WIDTH),
                _row_spec(tm, BRANCH_WIDTH)] + [_const_spec(a.shape) for a in consts]
    return pl.pallas_call(
        _mixer_post_kernel,
        grid=(s // tm,),
        in_specs=in_specs,
        out_specs=_row_spec(tm, D_MODEL),
        out_shape=jax.ShapeDtypeStruct((s, D_MODEL), F32),
        compiler_params=_params("arbitrary"),
        name="mixer_post",
    )(x, ya, yb, yc, *consts)


def _dense_ffn_kernel(x_ref, g_ref, w1_ref, w3_ref, w2_ref, o_ref):
    x = x_ref[...]
    h = _rms(x, g_ref[...]).astype(BF16)
    t = jax.nn.silu(_dot(h, w1_ref[...])) * _dot(h, w3_ref[...])
    o_ref[...] = x + _dot(t.astype(BF16), w2_ref[...])


def _dense_ffn(x, g, w1, w3, w2, tm):
    s = x.shape[0]
    consts = [g, w1, w3, w2]
    return pl.pallas_call(
        _dense_ffn_kernel,
        grid=(s // tm,),
        in_specs=[_row_spec(tm, D_MODEL)] + [_const_spec(a.shape) for a in consts],
        out_specs=_row_spec(tm, D_MODEL),
        out_shape=jax.ShapeDtypeStruct((s, D_MODEL), F32),
        compiler_params=_params("arbitrary"),
        name="dense_ffn",
    )(x, *consts)


def _route_kernel(x_ref, g_ref, wrt_ref, tri_ref, h_out, e_out, rank_out, wcol_out, cnt_out,
                  carry_scr):
    tm = x_ref.shape[0]

    @pl.when(pl.program_id(0) == 0)
    def _init():
        carry_scr[...] = jnp.zeros(carry_scr.shape, F32)

    h = _rms(x_ref[...], g_ref[...])
    h_out[...] = h
    logits = _dot_nt(wrt_ref[...], h.astype(BF16))[:N_EXPERTS, :]
    row = lax.broadcasted_iota(jnp.int32, logits.shape, 0)
    m1 = jnp.max(logits, axis=0, keepdims=True)
    i1 = jnp.min(jnp.where(logits == m1, row, N_EXPERTS), axis=0, keepdims=True)
    rest = jnp.where(row == i1, -jnp.inf, logits)
    m2 = jnp.max(rest, axis=0, keepdims=True)
    i2 = jnp.min(jnp.where(rest == m2, row, N_EXPERTS), axis=0, keepdims=True)
    e2 = jnp.exp(m2 - m1)
    den = 1.0 + e2
    oh1 = (row == i1).astype(F32)
    oh2 = (row == i2).astype(F32)
    cum = _dot(jnp.concatenate([oh1, oh2], axis=0).astype(BF16), tri_ref[...])
    tot = cum[:N_EXPERTS] + cum[N_EXPERTS:] + carry_scr[:, 0:1]
    rank1 = jnp.sum(oh1 * tot, axis=0, keepdims=True) - 1.0
    rank2 = jnp.sum(oh2 * tot, axis=0, keepdims=True) - 1.0
    e_out[...] = jnp.concatenate([i1, i2], axis=0)
    rank_out[...] = jnp.concatenate([rank1, rank2], axis=0).astype(jnp.int32)
    carry_scr[...] = jnp.broadcast_to(tot[:, tm - 1:tm], carry_scr.shape)
    cnt_out[...] = carry_scr[...]
    w_rows = jnp.concatenate([1.0 / den, e2 / den, jnp.zeros((LANE - 2, tm), F32)], axis=0)
    wcol_out[...] = w_rows.T


def _route(x, g, w_router_t, tm):
    s = x.shape[0]
    tri = (jnp.arange(tm)[:, None] <= jnp.arange(tm)[None, :]).astype(BF16)
    pair = lambda dt: jax.ShapeDtypeStruct((2, s), dt)
    return pl.pallas_call(
        _route_kernel,
        grid=(s // tm,),
        in_specs=[_row_spec(tm, D_MODEL), _const_spec(g.shape), _const_spec(w_router_t.shape),
                  _const_spec(tri.shape)],
        out_specs=[_row_spec(tm, D_MODEL), _col_spec(2, tm), _col_spec(2, tm), _row_spec(tm, LANE),
                   pl.BlockSpec((N_EXPERTS, LANE), lambda i: (0, 0))],
        out_shape=[jax.ShapeDtypeStruct((s, D_MODEL), F32), pair(jnp.int32), pair(jnp.int32),
                   jax.ShapeDtypeStruct((s, LANE), F32),
                   jax.ShapeDtypeStruct((N_EXPERTS, LANE), F32)],
        scratch_shapes=[pltpu.VMEM((N_EXPERTS, LANE), F32)],
        compiler_params=_params("arbitrary"),
        name="moe_route",
    )(x, g, w_router_t, tri)


def _row_copy(src_ref, src_row, dst_ref, dst_row, sem):
    return pltpu.make_async_copy(src_ref.at[pl.ds(src_row, 1)], dst_ref.at[pl.ds(dst_row, 1)], sem)


def _dispatch_kernel(pos_ref, h_ref, xs_in_ref, xs_ref, sem):
    del xs_in_ref
    tm = h_ref.shape[0]

    def issue(t, carry):
        for k in range(2):
            _row_copy(h_ref, t, xs_ref, pos_ref[k, t], sem).start()
        return carry

    lax.fori_loop(0, tm, issue, 0, unroll=8)
    for k in range(2):
        pltpu.make_async_copy(h_ref, xs_ref.at[pl.ds(0, tm)], sem).wait()


def _dispatch(pos, h, n_slots, tm):
    s = h.shape[0]
    xs0 = jnp.zeros((n_slots, D_MODEL), F32)
    return pl.pallas_call(
        _dispatch_kernel,
        grid=(s // tm,),
        in_specs=[pl.BlockSpec((2, tm), lambda i: (0, i), memory_space=pltpu.SMEM),
                  _row_spec(tm, D_MODEL), pl.BlockSpec(memory_space=pl.ANY)],
        out_specs=pl.BlockSpec(memory_space=pl.ANY),
        out_shape=jax.ShapeDtypeStruct((n_slots, D_MODEL), F32),
        scratch_shapes=[pltpu.SemaphoreType.DMA(())],
        input_output_aliases={2: 0},
        compiler_params=pltpu.CompilerParams(dimension_semantics=("arbitrary",),
                                             vmem_limit_bytes=VMEM_LIMIT, has_side_effects=True),
        name="moe_dispatch",
    )(pos, h, xs0)


def _expert_kernel(te_ref, nu_ref, xs_ref, w1_ref, w3_ref, w2_ref, ys_ref, h_scr):
    del te_ref
    c = pl.program_id(1)

    @pl.when(c == 0)
    def _load():
        h_scr[...] = xs_ref[...].astype(BF16)
        ys_ref[...] = jnp.zeros(ys_ref.shape, F32)

    @pl.when(pl.program_id(0) < nu_ref[0])
    def _ffn():
        h = h_scr[...]
        t = jax.nn.silu(_dot(h, w1_ref[...])) * _dot(h, w3_ref[...])
        ys_ref[...] += _dot(t.astype(BF16), w2_ref[...])


def _experts(tile_expert, n_used, xs, w1, w3, w2, j, tm, fc):
    n_slots = xs.shape[0]
    n_chunks = w1.shape[-1] // fc

    def chunk(i, c, nu):
        return jnp.where(i < nu[0], c, n_chunks - 1)

    grid_spec = pltpu.PrefetchScalarGridSpec(
        num_scalar_prefetch=2,
        grid=(n_slots // tm, n_chunks),
        in_specs=[
            pl.BlockSpec((tm, D_MODEL), lambda i, c, te, nu: (i, 0)),
            pl.BlockSpec((None, None, D_MODEL, fc),
                         lambda i, c, te, nu: (j, te[i], 0, chunk(i, c, nu))),
            pl.BlockSpec((None, None, D_MODEL, fc),
                         lambda i, c, te, nu: (j, te[i], 0, chunk(i, c, nu))),
            pl.BlockSpec((None, None, fc, D_MODEL),
                         lambda i, c, te, nu: (j, te[i], chunk(i, c, nu), 0)),
        ],
        out_specs=pl.BlockSpec((tm, D_MODEL), lambda i, c, te, nu: (i, 0)),
        scratch_shapes=[pltpu.VMEM((tm, D_MODEL), BF16)],
    )
    return pl.pallas_call(
        _expert_kernel,
        grid_spec=grid_spec,
        out_shape=jax.ShapeDtypeStruct((n_slots, D_MODEL), F32),
        compiler_params=_params("arbitrary", "arbitrary"),
        name="moe_experts",
    )(tile_expert, n_used, xs, w1, w3, w2)


def _combine_kernel(pos_ref, x_ref, wcol_ref, ys_ref, o_ref, buf, sem):
    tm = x_ref.shape[0]

    def issue(t, carry):
        for k in range(2):
            _row_copy(ys_ref, pos_ref[k, t], buf.at[k], t, sem).start()
        return carry

    lax.fori_loop(0, tm, issue, 0, unroll=8)
    for k in range(2):
        pltpu.make_async_copy(ys_ref.at[pl.ds(0, tm)], buf.at[k], sem).wait()
    w = wcol_ref[...]
    o_ref[...] = x_ref[...] + w[:, 0:1] * buf[0] + w[:, 1:2] * buf[1]


def _combine(pos, x, wcol, ys, tm):
    s = x.shape[0]
    return pl.pallas_call(
        _combine_kernel,
        grid=(s // tm,),
        in_specs=[pl.BlockSpec((2, tm), lambda i: (0, i), memory_space=pltpu.SMEM),
                  _row_spec(tm, D_MODEL), _row_spec(tm, LANE), pl.BlockSpec(memory_space=pl.ANY)],
        out_specs=_row_spec(tm, D_MODEL),
        out_shape=jax.ShapeDtypeStruct((s, D_MODEL), F32),
        scratch_shapes=[pltpu.VMEM((2, tm, D_MODEL), F32), pltpu.SemaphoreType.DMA(())],
        compiler_params=_params("arbitrary"),
        name="moe_combine",
    )(pos, x, wcol, ys)


def _moe(x, g, w_router, w1, w3, w2, j, tm, fc):
    s = x.shape[0]
    n_tiles = 2 * s // tm + N_EXPERTS
    wrt = jnp.pad(w_router.T, ((0, 16 - N_EXPERTS), (0, 0))).astype(BF16)
    h, e, rank, wcol, cnt = _route(x, g, wrt, tm)
    counts = cnt[:, 0].astype(jnp.int32)
    padded = (counts + tm - 1) // tm * tm
    ends = jnp.cumsum(padded)
    offs = ends - padded
    pos = rank
    for ex in range(N_EXPERTS):
        pos = pos + jnp.where(e == ex, offs[ex], 0)
    tile_start = jnp.arange(n_tiles, dtype=jnp.int32) * tm
    tile_expert = jnp.minimum(jnp.sum(tile_start[:, None] >= ends[None, :], axis=1),
                              N_EXPERTS - 1).astype(jnp.int32)
    n_used = (ends[-1:] // tm).astype(jnp.int32)
    xs = _dispatch(pos, h, n_tiles * tm, tm)
    ys = _experts(tile_expert, n_used, xs, w1, w3, w2, j, tm, fc)
    return _combine(pos, x, wcol, ys, tm)


def _final_norm_kernel(x_ref, g_ref, o_ref):
    o_ref[...] = _rms(x_ref[...], g_ref[...])


def _final_norm(x, g, tm):
    s = x.shape[0]
    return pl.pallas_call(
        _final_norm_kernel,
        grid=(s // tm,),
        in_specs=[_row_spec(tm, D_MODEL), _const_spec(g.shape)],
        out_specs=_row_spec(tm, D_MODEL),
        out_shape=jax.ShapeDtypeStruct((s, D_MODEL), F32),
        compiler_params=_params("arbitrary"),
        name="final_norm",
    )(x, g)


def _pad_heads(w, width, padded=HEAD_PAD):
    r = w.shape[0]
    w = w.reshape(r, MLA_HEADS, width)
    return jnp.pad(w, ((0, 0), (0, 0), (0, padded - width))).reshape(r, MLA_HEADS * padded)


def _rotate_half_cols(w_rope):
    half = MLA_ROPE // 2
    return jnp.concatenate([-w_rope[..., half:], w_rope[..., :half]], axis=-1)


_W_IN_SEGMENTS = (("w_dq", Q_RANK), ("w_dkv", KV_RANK), ("w_kr", MLA_ROPE), ("w_sg", 2 * SG_WIDTH),
                  ("w_xq", X_HEADS * X_HEAD_DIM), ("w_gate", N_BRANCH * D_MODEL))
_W_IN_SPLIT = tuple(name for name, _ in _W_IN_SEGMENTS if name != "w_kr")


def _split_w_in_kernel(w_ref, *outs):
    start = 0
    out = iter(outs)
    for name, width in _W_IN_SEGMENTS:
        if name in _W_IN_SPLIT:
            next(out)[...] = w_ref[:, start:start + width].astype(BF16)
        else:
            next(out)[...] = w_ref[:, start:start + LANE]
        start += width


def _split_w_in(w_in, l):
    _, d, total = w_in.shape
    rows = 256
    out_shape = [jax.ShapeDtypeStruct((d, width), BF16) if name in _W_IN_SPLIT else
                 jax.ShapeDtypeStruct((d, LANE), F32) for name, width in _W_IN_SEGMENTS]
    outs = pl.pallas_call(
        _split_w_in_kernel,
        grid=(d // rows,),
        in_specs=[pl.BlockSpec((None, rows, total), lambda i: (l, i, 0))],
        out_specs=[_row_spec(rows, o.shape[1]) for o in out_shape],
        out_shape=out_shape,
        compiler_params=_params("arbitrary"),
        name="split_w_in",
    )(w_in)
    return dict(zip((name for name, _ in _W_IN_SEGMENTS), outs))


def _layer_weights(l, g_mix, w_in, g_q, g_kv, w_uq, w_uk, w_uv, g_sg, w_s, b_s, w_br,
                   b_gate, w_out):
    split = _split_w_in(w_in, l)
    w_kr = split.pop("w_kr")[:, :MLA_ROPE]
    rope_pad = ((0, 0), (MLA_NOPE, HEAD_PAD - MLA_QK))
    w_kr2 = jnp.concatenate([jnp.pad(w_kr, rope_pad), jnp.pad(_rotate_half_cols(w_kr), rope_pad)],
                            axis=1)
    uq = w_uq[l].reshape(Q_RANK, MLA_HEADS, MLA_QK)
    uq_rot = _rotate_half_cols(uq[..., MLA_NOPE:]).reshape(Q_RANK, MLA_HEADS * MLA_ROPE)
    w_br0 = jnp.pad(w_br[l, 0].reshape(MLA_HEADS, MLA_V, D_MODEL),
                    ((0, 0), (0, HEAD_PAD - MLA_V), (0, 0))).reshape(QK_PAD, D_MODEL)
    return {
        "g_mix": g_mix[l][None, :],
        **split,
        "w_kr": w_kr2.astype(BF16),
        "g_q": g_q[l][None, :],
        "g_kv": g_kv[l][None, :],
        "w_uq_t": _pad_heads(w_uq[l], MLA_QK).T.astype(BF16),
        "w_uqr_t": uq_rot.T.astype(BF16),
        "w_uk": _pad_heads(w_uk[l], MLA_NOPE).astype(BF16),
        "w_uv_t": _pad_heads(w_uv[l], MLA_V, V_ROWS).T.astype(BF16),
        "g_sg": g_sg[l][None, :],
        "w_s": w_s[l].astype(BF16),
        "b_st": b_s[l].T,
        "b_gate": b_gate[l],
        "w_br0": w_br0.astype(BF16),
        "w_br1": w_br[l, 1].astype(BF16),
        "w_br2": w_br[l, 2].astype(BF16),
        "w_out": w_out[l].astype(BF16),
    }


def _rope_constants():
    half = MLA_ROPE // 2
    inv_freq = 1.0 / (ROPE_THETA ** (jnp.arange(0, MLA_ROPE, 2, dtype=F32) / MLA_ROPE))
    lane_freq = jnp.zeros((HEAD_PAD,), F32)
    lane_freq = lane_freq.at[MLA_NOPE:MLA_NOPE + half].set(inv_freq)
    lane_freq = lane_freq.at[MLA_NOPE + half:MLA_QK].set(inv_freq)
    v_one = jnp.zeros((MLA_HEADS, V_ROWS), F32).at[:, MLA_V].set(1.0).reshape(1, MLA_HEADS * V_ROWS)
    return lane_freq[None, :], v_one


def _tiles(s):
    tm = min(512, s)
    tq = min(512, s)
    tk = min(512, s)
    tmoe = min(512, s)
    return tm, tq, tk, tmoe


def kernel(x, mem, positions, g_mix, w_in, g_q, g_kv, w_uq, w_uk, w_uv, g_sg, w_s, b_s, g_mem, w_mkv, w_br, b_gate, w_out, g_ffn, dense_w1, dense_w3, dense_w2, w_router, moe_w1, moe_w3, moe_w2, g_final):
    b, s, d = x.shape
    assert b == 1 and d == D_MODEL and s % SG_CHUNK == 0
    depth = g_mix.shape[0]
    tm, tq, tk, tmoe = _tiles(s)
    xs = x[0]
    pos = positions[0].astype(F32)[:, None]
    inv_freq, v_one = _rope_constants()
    rope = _rope_tables(pos, inv_freq, tm)
    moe_bf16 = [w.astype(BF16) for w in (moe_w1, moe_w3, moe_w2)]
    for l in range(depth):
        lw = _layer_weights(l, g_mix, w_in, g_q, g_kv, w_uq, w_uk, w_uv, g_sg, w_s, b_s, w_br,
                            b_gate, w_out)
        kv_mem = _mem_kv(mem[0], g_mem[None, :], w_mkv[l].astype(BF16))
        lw["k_mem_t"] = kv_mem[:, :BRANCH_WIDTH].T
        lw["v_mem"] = kv_mem[:, BRANCH_WIDTH:]
        lw["v_one"] = v_one
        qt, k, vt, yb, yc = _mixer_pre(xs, rope, lw, tm)
        ya = _flash(qt, k, vt, tq, tk)
        xs = _mixer_post(xs, ya, yb, yc, lw, tm)
        j = l // 2
        if l % 2 == 0:
            xs = _dense_ffn(xs, g_ffn[l][None, :], dense_w1[j].astype(BF16),
                            dense_w3[j].astype(BF16), dense_w2[j].astype(BF16), tm)
        else:
            fc = moe_w1.shape[-1] // 2
            xs = _moe(xs, g_ffn[l][None, :], w_router[j], *moe_bf16, j, tmoe, fc)
    return _final_norm(xs, g_final[None, :], tm)[None]
```

```python
import functools

import jax
import jax.numpy as jnp
from jax import lax
from jax.experimental import pallas as pl
from jax.experimental.pallas import tpu as pltpu

D_MODEL = 1024
N_BRANCH = 3
BRANCH_WIDTH = 512
MLA_HEADS = 8
MLA_NOPE = 64
MLA_ROPE = 32
MLA_QK = MLA_NOPE + MLA_ROPE
MLA_V = BRANCH_WIDTH // MLA_HEADS
Q_RANK = 384
KV_RANK = 256
ROPE_THETA = 10000.0
SG_CHUNK = 128
SG_GROUPS = 4
SG_WIDTH = BRANCH_WIDTH
X_HEADS = 4
X_HEAD_DIM = BRANCH_WIDTH // X_HEADS
N_EXPERTS = 8
EPS = 1e-6
LOG2_E = 1.4426950408889634

LANE = 128
HEAD_PAD = LANE
QK_PAD = MLA_HEADS * HEAD_PAD
V_ROWS = 80
VMEM_LIMIT = 56 * 1024 * 1024
FLASH_QBLOCKS = 4
FLASH_AHEAD = 2
FLASH_UNROLL = 12

F32 = jnp.float32
BF16 = jnp.bfloat16


def _dot(a, b):
    return jnp.dot(a, b, preferred_element_type=F32)


def _dot_nt(a, b):
    return lax.dot_general(a, b, (((1,), (1,)), ((), ())), preferred_element_type=F32)


def _rms(x, g):
    return x * lax.rsqrt(jnp.mean(x * x, axis=-1, keepdims=True) + EPS) * g


def _const_spec(shape):
    nd = len(shape)
    return pl.BlockSpec(shape, lambda *_: (0,) * nd, pipeline_mode=pl.Buffered(1))


def _row_spec(tm, width):
    return pl.BlockSpec((tm, width), lambda i: (i, 0))


def _params(*sem):
    return pltpu.CompilerParams(dimension_semantics=sem, vmem_limit_bytes=VMEM_LIMIT)


def _mem_kv_kernel(mem_ref, g_ref, w_ref, o_ref):
    mem_n = _rms(mem_ref[...], g_ref[...]).astype(BF16)
    o_ref[...] = _dot(mem_n, w_ref[...]).astype(BF16)


def _mem_kv(mem, g_mem, w_mkv):
    m = mem.shape[0]
    return pl.pallas_call(
        _mem_kv_kernel,
        out_shape=jax.ShapeDtypeStruct((m, 2 * BRANCH_WIDTH), BF16),
        name="mem_kv",
    )(mem, g_mem, w_mkv)


def _rope_tables_kernel(pos_ref, posr_ref, invf_ref, invfc_ref, cs_out, sn_out, cst_out, snt_out):
    ang = pos_ref[...] * invf_ref[...]
    cs_out[...] = jnp.cos(ang)
    sn_out[...] = jnp.sin(ang)
    ang_t = invfc_ref[...] * posr_ref[...]
    cst_out[...] = jnp.cos(ang_t)
    snt_out[...] = jnp.sin(ang_t)


def _rope_tables(pos, inv_freq, tm):
    s = pos.shape[0]
    row = jax.ShapeDtypeStruct((s, HEAD_PAD), F32)
    col = jax.ShapeDtypeStruct((HEAD_PAD, s), F32)
    return pl.pallas_call(
        _rope_tables_kernel,
        grid=(s // tm,),
        in_specs=[_row_spec(tm, 1), pl.BlockSpec((1, tm), lambda i: (0, i)),
                  _const_spec(inv_freq.shape), _const_spec(inv_freq.T.shape)],
        out_specs=[_row_spec(tm, HEAD_PAD)] * 2 + [pl.BlockSpec((HEAD_PAD, tm), lambda i: (0, i))] * 2,
        out_shape=[row, row, col, col],
        compiler_params=_params("arbitrary"),
        name="rope_tables",
    )(pos, pos.reshape(1, s), inv_freq, inv_freq.T)


def _mixer_pre_kernel(x_ref, cs_ref, sn_ref, cst_ref, snt_ref, gmix_ref, wdq_ref, wdkv_ref,
                      wkr_ref, wsg_ref, wxq_ref, gq_ref, gkv_ref, wuqt_ref, wuqrt_ref, wuk_ref,
                      wuvt_ref, vonec_ref, gsg_ref, ws_ref, bst_ref, kmt_ref, vm_ref,
                      qt_out, k_out, vt_out, yb_out, yc_out):
    tm = x_ref.shape[0]
    h = _rms(x_ref[...], gmix_ref[...]).astype(BF16)

    cq = _rms(_dot(h, wdq_ref[...]), gq_ref[...]).astype(BF16)
    qa_t = _dot_nt(wuqt_ref[...], cq)
    qb_t = _dot_nt(wuqrt_ref[...], cq)
    ckv = _rms(_dot(h, wdkv_ref[...]), gkv_ref[...]).astype(BF16)
    kn = _dot(ckv, wuk_ref[...])
    kr = _dot(h, wkr_ref[...])
    k_rope = kr[:, :LANE] * cs_ref[...] + kr[:, LANE:] * sn_ref[...]
    scale = MLA_QK ** -0.5 * LOG2_E
    cs_r = cst_ref[MLA_NOPE:MLA_QK, :]
    sn_r = snt_ref[MLA_NOPE:MLA_QK, :]
    for hd in range(MLA_HEADS):
        sl = slice(hd * HEAD_PAD, (hd + 1) * HEAD_PAD)
        qa = qa_t[sl, :]
        rope = qa[MLA_NOPE:MLA_QK, :] * cs_r + qb_t[hd * MLA_ROPE:(hd + 1) * MLA_ROPE, :] * sn_r
        q_head = jnp.concatenate([qa[:MLA_NOPE, :], rope, qa[MLA_QK:, :]], axis=0)
        qt_out[sl, :] = (q_head * scale).astype(BF16)
        k_out[:, sl] = (kn[:, sl] + k_rope).astype(BF16)
    vt_out[...] = (_dot_nt(wuvt_ref[...], ckv) + vonec_ref[...]).astype(BF16)

    z = jax.nn.gelu(_dot(h, wsg_ref[...]))
    u = z[:, :SG_WIDTH]
    vn = _rms(z[:, SG_WIDTH:], gsg_ref[...]).astype(BF16)
    gw = SG_WIDTH // SG_GROUPS
    for n in range(tm // SG_CHUNK):
        rows = slice(n * SG_CHUNK, (n + 1) * SG_CHUNK)
        for g in range(SG_GROUPS):
            cols = slice(g * gw, (g + 1) * gw)
            sg = _dot(ws_ref[g], vn[rows, cols]) + bst_ref[:, g:g + 1]
            yb_out[rows, cols] = (u[rows, cols] * sg).astype(BF16)

    qx = _dot(h, wxq_ref[...]) * (X_HEAD_DIM ** -0.5)
    for hd in range(X_HEADS):
        sl = slice(hd * X_HEAD_DIM, (hd + 1) * X_HEAD_DIM)
        sc = _dot(qx[:, sl].astype(BF16), kmt_ref[sl, :])
        p = jnp.exp(sc - jnp.max(sc, axis=-1, keepdims=True))
        l = jnp.sum(p, axis=-1, keepdims=True)
        o = _dot(p.astype(BF16), vm_ref[:, sl])
        yc_out[:, sl] = (o / l).astype(BF16)


def _col_spec(height, tm):
    return pl.BlockSpec((height, tm), lambda i: (0, i))


def _mixer_pre(x, rope, lw, tm):
    s = x.shape[0]
    ins = [x, *rope, lw["g_mix"], lw["w_dq"], lw["w_dkv"], lw["w_kr"], lw["w_sg"],
           lw["w_xq"], lw["g_q"], lw["g_kv"], lw["w_uq_t"], lw["w_uqr_t"], lw["w_uk"],
           lw["w_uv_t"], lw["v_one"].T, lw["g_sg"], lw["w_s"],
           lw["b_st"], lw["k_mem_t"], lw["v_mem"]]
    in_specs = [_row_spec(tm, D_MODEL), _row_spec(tm, HEAD_PAD), _row_spec(tm, HEAD_PAD),
                _col_spec(HEAD_PAD, tm), _col_spec(HEAD_PAD, tm)] + \
               [_const_spec(a.shape) for a in ins[5:]]
    v_rows = MLA_HEADS * V_ROWS
    out_shape = [jax.ShapeDtypeStruct((QK_PAD, s), BF16), jax.ShapeDtypeStruct((s, QK_PAD), BF16),
                 jax.ShapeDtypeStruct((v_rows, s), BF16)] + \
                [jax.ShapeDtypeStruct((s, BRANCH_WIDTH), BF16)] * 2
    out_specs = [_col_spec(QK_PAD, tm), _row_spec(tm, QK_PAD), _col_spec(v_rows, tm)] + \
                [_row_spec(tm, BRANCH_WIDTH)] * 2
    return pl.pallas_call(
        _mixer_pre_kernel,
        grid=(s // tm,),
        in_specs=in_specs,
        out_specs=out_specs,
        out_shape=out_shape,
        compiler_params=_params("arbitrary"),
        name="mixer_pre",
    )(*ins)


def _flash_kernel(qt_ref, k_ref, vt_ref, o_ref, s_scr, m_scr, acc_scr, *, tq, tk):
    n = k_ref.shape[0] // tk
    groups = qt_ref.shape[1] // tq

    def scores(g, t, slot):
        start = pl.multiple_of(t * tk, tk)
        s_scr[slot] = _dot(k_ref[pl.ds(start, tk), :], qt_ref[:, g * tq:(g + 1) * tq])

    def update(g, t, slot):
        start = pl.multiple_of(t * tk, tk)
        sc = s_scr[slot]
        m = m_scr[g]
        m_new = jnp.maximum(m, jnp.max(sc, axis=0, keepdims=True))
        alpha = jnp.exp2(m - m_new)
        p = jnp.exp2(sc - m_new).astype(BF16)
        acc_scr[g] = acc_scr[g] * alpha + _dot(vt_ref[:, pl.ds(start, tk)], p)
        m_scr[g] = m_new

    slots = FLASH_AHEAD + 1

    def scores_at(pos):
        g, t = divmod(pos, n)
        if g < groups:
            scores(g, t, pos % slots)

    def step(g, t):
        scores_at(g * n + t + FLASH_AHEAD)
        update(g, t, (g * n + t) % slots)

    def finish(g):
        acc = acc_scr[g]
        out_t = acc / acc[MLA_V:MLA_V + 1, :]
        pad = jnp.zeros((HEAD_PAD - V_ROWS, tq), F32)
        o_ref[g * tq:(g + 1) * tq, :] = jnp.concatenate([out_t, pad], axis=0).T.astype(BF16)

    m_scr[...] = jnp.full(m_scr.shape, -jnp.inf, F32)
    acc_scr[...] = jnp.zeros(acc_scr.shape, F32)
    for pos in range(FLASH_AHEAD):
        scores_at(pos)
    loops = max(n - 1 - FLASH_AHEAD, 0) // FLASH_UNROLL
    for g in range(groups):
        off = g * n
        step(g, 0)
        if g > 0:
            finish(g - 1)

        def body(u, carry, g=g, off=off):
            base = 1 + u * FLASH_UNROLL
            for r in range(FLASH_UNROLL):
                scores(g, base + r + FLASH_AHEAD, (off + 1 + r + FLASH_AHEAD) % slots)
                update(g, base + r, (off + 1 + r) % slots)
            return carry

        lax.fori_loop(0, loops, body, 0)
        for t in range(1 + loops * FLASH_UNROLL, n):
            step(g, t)
    finish(groups - 1)


def _flash(qt, k, vt, tq, tk):
    s = k.shape[0]
    groups = FLASH_QBLOCKS if s % (FLASH_QBLOCKS * tq) == 0 else 1
    tqg = groups * tq
    return pl.pallas_call(
        functools.partial(_flash_kernel, tq=tq, tk=tk),
        grid=(MLA_HEADS, s // tqg),
        in_specs=[pl.BlockSpec((HEAD_PAD, tqg), lambda h, i: (h, i)),
                  pl.BlockSpec((s, HEAD_PAD), lambda h, i: (0, h)),
                  pl.BlockSpec((V_ROWS, s), lambda h, i: (h, 0))],
        out_specs=pl.BlockSpec((tqg, HEAD_PAD), lambda h, i: (i, h)),
        out_shape=jax.ShapeDtypeStruct((s, QK_PAD), BF16),
        scratch_shapes=[pltpu.VMEM((FLASH_AHEAD + 1, tk, tq), F32),
                        pltpu.VMEM((groups, 1, tq), F32),
                        pltpu.VMEM((groups, V_ROWS, tq), F32)],
        compiler_params=_params("arbitrary", "arbitrary"),
        name="flash",
    )(qt, k, vt)


def _mixer_post_kernel(x_ref, ya_ref, yb_ref, yc_ref, gmix_ref, wgate_ref, bgate_ref,
                       wbr0_ref, wbr1_ref, wbr2_ref, wout_ref, o_ref):
    x = x_ref[...]
    h = _rms(x, gmix_ref[...]).astype(BF16)
    mix = None
    for n, (y_ref, w_ref) in enumerate(((ya_ref, wbr0_ref), (yb_ref, wbr1_ref), (yc_ref, wbr2_ref))):
        cols = slice(n * D_MODEL, (n + 1) * D_MODEL)
        gate = jax.nn.sigmoid(_dot(h, wgate_ref[:, cols]) + bgate_ref[n:n + 1, :])
        term = gate * _dot(y_ref[...], w_ref[...])
        mix = term if mix is None else mix + term
    o_ref[...] = x + _dot(mix.astype(BF16), wout_ref[...])


def _mixer_post(x, ya, yb, yc, lw, tm):
    s = x.shape[0]
    consts = [lw["g_mix"], lw["w_gate"], lw["b_gate"], lw["w_br0"], lw["w_br1"], lw["w_br2"],
              lw["w_out"]]
    in_specs = [_row_spec(tm, D_MODEL), _row_spec(tm, QK_PAD), _row_spec(tm, BRANCH_WIDTH),
                _row_spec(tm, BRANCH_WIDTH)] + [_const_spec(a.shape) for a in consts]
    return pl.pallas_call(
        _mixer_post_kernel,
        grid=(s // tm,),
        in_specs=in_specs,
        out_specs=_row_spec(tm, D_MODEL),
        out_shape=jax.ShapeDtypeStruct((s, D_MODEL), F32),
        compiler_params=_params("arbitrary"),
        name="mixer_post",
    )(x, ya, yb, yc, *consts)


def _dense_ffn_kernel(x_ref, g_ref, w1_ref, w3_ref, w2_ref, o_ref):
    x = x_ref[...]
    h = _rms(x, g_ref[...]).astype(BF16)
    t = jax.nn.silu(_dot(h, w1_ref[...])) * _dot(h, w3_ref[...])
    o_ref[...] = x + _dot(t.astype(BF16), w2_ref[...])


def _dense_ffn(x, g, w1, w3, w2, tm):
    s = x.shape[0]
    consts = [g, w1, w3, w2]
    return pl.pallas_call(
        _dense_ffn_kernel,
        grid=(s // tm,),
        in_specs=[_row_spec(tm, D_MODEL)] + [_const_spec(a.shape) for a in consts],
        out_specs=_row_spec(tm, D_MODEL),
        out_shape=jax.ShapeDtypeStruct((s, D_MODEL), F32),
        compiler_params=_params("arbitrary"),
        name="dense_ffn",
    )(x, *consts)


def _route_kernel(x_ref, g_ref, wrt_ref, tri_ref, h_out, e_out, rank_out, wcol_out, cnt_out,
                  carry_scr):
    tm = x_ref.shape[0]

    @pl.when(pl.program_id(0) == 0)
    def _init():
        carry_scr[...] = jnp.zeros(carry_scr.shape, F32)

    h = _rms(x_ref[...], g_ref[...])
    h_out[...] = h
    logits = _dot_nt(wrt_ref[...], h.astype(BF16))[:N_EXPERTS, :]
    row = lax.broadcasted_iota(jnp.int32, logits.shape, 0)
    m1 = jnp.max(logits, axis=0, keepdims=True)
    i1 = jnp.min(jnp.where(logits == m1, row, N_EXPERTS), axis=0, keepdims=True)
    rest = jnp.where(row == i1, -jnp.inf, logits)
    m2 = jnp.max(rest, axis=0, keepdims=True)
    i2 = jnp.min(jnp.where(rest == m2, row, N_EXPERTS), axis=0, keepdims=True)
    e2 = jnp.exp(m2 - m1)
    den = 1.0 + e2
    oh1 = (row == i1).astype(F32)
    oh2 = (row == i2).astype(F32)
    cum = _dot(jnp.concatenate([oh1, oh2], axis=0).astype(BF16), tri_ref[...])
    tot = cum[:N_EXPERTS] + cum[N_EXPERTS:] + carry_scr[:, 0:1]
    rank1 = jnp.sum(oh1 * tot, axis=0, keepdims=True) - 1.0
    rank2 = jnp.sum(oh2 * tot, axis=0, keepdims=True) - 1.0
    e_out[...] = jnp.concatenate([i1, i2], axis=0)
    rank_out[...] = jnp.concatenate([rank1, rank2], axis=0).astype(jnp.int32)
    carry_scr[...] = jnp.broadcast_to(tot[:, tm - 1:tm], carry_scr.shape)
    cnt_out[...] = carry_scr[...]
    w_rows = jnp.concatenate([1.0 / den, e2 / den, jnp.zeros((LANE - 2, tm), F32)], axis=0)
    wcol_out[...] = w_rows.T


def _route(x, g, w_router_t, tm):
    s = x.shape[0]
    tri = (jnp.arange(tm)[:, None] <= jnp.arange(tm)[None, :]).astype(BF16)
    pair = lambda dt: jax.ShapeDtypeStruct((2, s), dt)
    return pl.pallas_call(
        _route_kernel,
        grid=(s // tm,),
        in_specs=[_row_spec(tm, D_MODEL), _const_spec(g.shape), _const_spec(w_router_t.shape),
                  _const_spec(tri.shape)],
        out_specs=[_row_spec(tm, D_MODEL), _col_spec(2, tm), _col_spec(2, tm), _row_spec(tm, LANE),
                   pl.BlockSpec((N_EXPERTS, LANE), lambda i: (0, 0))],
        out_shape=[jax.ShapeDtypeStruct((s, D_MODEL), F32), pair(jnp.int32), pair(jnp.int32),
                   jax.ShapeDtypeStruct((s, LANE), F32),
                   jax.ShapeDtypeStruct((N_EXPERTS, LANE), F32)],
        scratch_shapes=[pltpu.VMEM((N_EXPERTS, LANE), F32)],
        compiler_params=_params("arbitrary"),
        name="moe_route",
    )(x, g, w_router_t, tri)


def _row_copy(src_ref, src_row, dst_ref, dst_row, sem):
    return pltpu.make_async_copy(src_ref.at[pl.ds(src_row, 1)], dst_ref.at[pl.ds(dst_row, 1)], sem)


def _dispatch_kernel(pos_ref, h_ref, xs_in_ref, xs_ref, sem):
    del xs_in_ref
    tm = h_ref.shape[0]

    def issue(t, carry):
        for k in range(2):
            _row_copy(h_ref, t, xs_ref, pos_ref[k, t], sem).start()
        return carry

    lax.fori_loop(0, tm, issue, 0, unroll=8)
    for k in range(2):
        pltpu.make_async_copy(h_ref, xs_ref.at[pl.ds(0, tm)], sem).wait()


def _dispatch(pos, h, n_slots, tm):
    s = h.shape[0]
    xs0 = jnp.zeros((n_slots, D_MODEL), F32)
    return pl.pallas_call(
        _dispatch_kernel,
        grid=(s // tm,),
        in_specs=[pl.BlockSpec((2, tm), lambda i: (0, i), memory_space=pltpu.SMEM),
                  _row_spec(tm, D_MODEL), pl.BlockSpec(memory_space=pl.ANY)],
        out_specs=pl.BlockSpec(memory_space=pl.ANY),
        out_shape=jax.ShapeDtypeStruct((n_slots, D_MODEL), F32),
        scratch_shapes=[pltpu.SemaphoreType.DMA(())],
        input_output_aliases={2: 0},
        compiler_params=pltpu.CompilerParams(dimension_semantics=("arbitrary",),
                                             vmem_limit_bytes=VMEM_LIMIT, has_side_effects=True),
        name="moe_dispatch",
    )(pos, h, xs0)


def _expert_kernel(te_ref, nu_ref, xs_ref, w1_ref, w3_ref, w2_ref, ys_ref, h_scr):
    del te_ref
    c = pl.program_id(1)

    @pl.when(c == 0)
    def _load():
        h_scr[...] = xs_ref[...].astype(BF16)
        ys_ref[...] = jnp.zeros(ys_ref.shape, F32)

    @pl.when(pl.program_id(0) < nu_ref[0])
    def _ffn():
        h = h_scr[...]
        t = jax.nn.silu(_dot(h, w1_ref[...])) * _dot(h, w3_ref[...])
        ys_ref[...] += _dot(t.astype(BF16), w2_ref[...])


def _experts(tile_expert, n_used, xs, w1, w3, w2, j, tm, fc):
    n_slots = xs.shape[0]
    n_chunks = w1.shape[-1] // fc

    def chunk(i, c, nu):
        return jnp.where(i < nu[0], c, n_chunks - 1)

    grid_spec = pltpu.PrefetchScalarGridSpec(
        num_scalar_prefetch=2,
        grid=(n_slots // tm, n_chunks),
        in_specs=[
            pl.BlockSpec((tm, D_MODEL), lambda i, c, te, nu: (i, 0)),
            pl.BlockSpec((None, None, D_MODEL, fc),
                         lambda i, c, te, nu: (j, te[i], 0, chunk(i, c, nu))),
            pl.BlockSpec((None, None, D_MODEL, fc),
                         lambda i, c, te, nu: (j, te[i], 0, chunk(i, c, nu))),
            pl.BlockSpec((None, None, fc, D_MODEL),
                         lambda i, c, te, nu: (j, te[i], chunk(i, c, nu), 0)),
        ],
        out_specs=pl.BlockSpec((tm, D_MODEL), lambda i, c, te, nu: (i, 0)),
        scratch_shapes=[pltpu.VMEM((tm, D_MODEL), BF16)],
    )
    return pl.pallas_call(
        _expert_kernel,
        grid_spec=grid_spec,
        out_shape=jax.ShapeDtypeStruct((n_slots, D_MODEL), F32),
        compiler_params=_params("arbitrary", "arbitrary"),
        name="moe_experts",
    )(tile_expert, n_used, xs, w1, w3, w2)


def _combine_kernel(pos_ref, x_ref, wcol_ref, ys_ref, o_ref, buf, sem):
    tm = x_ref.shape[0]

    def issue(t, carry):
        for k in range(2):
            _row_copy(ys_ref, pos_ref[k, t], buf.at[k], t, sem).start()
        return carry

    lax.fori_loop(0, tm, issue, 0, unroll=8)
    for k in range(2):
        pltpu.make_async_copy(ys_ref.at[pl.ds(0, tm)], buf.at[k], sem).wait()
    w = wcol_ref[...]
    o_ref[...] = x_ref[...] + w[:, 0:1] * buf[0] + w[:, 1:2] * buf[1]


def _combine(pos, x, wcol, ys, tm):
    s = x.shape[0]
    return pl.pallas_call(
        _combine_kernel,
        grid=(s // tm,),
        in_specs=[pl.BlockSpec((2, tm), lambda i: (0, i), memory_space=pltpu.SMEM),
                  _row_spec(tm, D_MODEL), _row_spec(tm, LANE), pl.BlockSpec(memory_space=pl.ANY)],
        out_specs=_row_spec(tm, D_MODEL),
        out_shape=jax.ShapeDtypeStruct((s, D_MODEL), F32),
        scratch_shapes=[pltpu.VMEM((2, tm, D_MODEL), F32), pltpu.SemaphoreType.DMA(())],
        compiler_params=_params("arbitrary"),
        name="moe_combine",
    )(pos, x, wcol, ys)


def _moe(x, g, w_router, w1, w3, w2, j, tm, fc):
    s = x.shape[0]
    n_tiles = 2 * s // tm + N_EXPERTS
    wrt = jnp.pad(w_router.T, ((0, 16 - N_EXPERTS), (0, 0))).astype(BF16)
    h, e, rank, wcol, cnt = _route(x, g, wrt, tm)
    counts = cnt[:, 0].astype(jnp.int32)
    padded = (counts + tm - 1) // tm * tm
    ends = jnp.cumsum(padded)
    offs = ends - padded
    pos = rank
    for ex in range(N_EXPERTS):
        pos = pos + jnp.where(e == ex, offs[ex], 0)
    tile_start = jnp.arange(n_tiles, dtype=jnp.int32) * tm
    tile_expert = jnp.minimum(jnp.sum(tile_start[:, None] >= ends[None, :], axis=1),
                              N_EXPERTS - 1).astype(jnp.int32)
    n_used = (ends[-1:] // tm).astype(jnp.int32)
    xs = _dispatch(pos, h, n_tiles * tm, tm)
    ys = _experts(tile_expert, n_used, xs, w1, w3, w2, j, tm, fc)
    return _combine(pos, x, wcol, ys, tm)


def _final_norm_kernel(x_ref, g_ref, o_ref):
    o_ref[...] = _rms(x_ref[...], g_ref[...])


def _final_norm(x, g, tm):
    s = x.shape[0]
    return pl.pallas_call(
        _final_norm_kernel,
        grid=(s // tm,),
        in_specs=[_row_spec(tm, D_MODEL), _const_spec(g.shape)],
        out_specs=_row_spec(tm, D_MODEL),
        out_shape=jax.ShapeDtypeStruct((s, D_MODEL), F32),
        compiler_params=_params("arbitrary"),
        name="final_norm",
    )(x, g)


def _pad_heads(w, width, padded=HEAD_PAD):
    r = w.shape[0]
    w = w.reshape(r, MLA_HEADS, width)
    return jnp.pad(w, ((0, 0), (0, 0), (0, padded - width))).reshape(r, MLA_HEADS * padded)


def _rotate_half_cols(w_rope):
    half = MLA_ROPE // 2
    return jnp.concatenate([-w_rope[..., half:], w_rope[..., :half]], axis=-1)


_W_IN_SEGMENTS = (("w_dq", Q_RANK), ("w_dkv", KV_RANK), ("w_kr", MLA_ROPE), ("w_sg", 2 * SG_WIDTH),
                  ("w_xq", X_HEADS * X_HEAD_DIM), ("w_gate", N_BRANCH * D_MODEL))
_W_IN_SPLIT = tuple(name for name, _ in _W_IN_SEGMENTS if name != "w_kr")


def _split_w_in_kernel(w_ref, *outs):
    start = 0
    out = iter(outs)
    for name, width in _W_IN_SEGMENTS:
        if name in _W_IN_SPLIT:
            next(out)[...] = w_ref[:, start:start + width].astype(BF16)
        else:
            next(out)[...] = w_ref[:, start:start + LANE]
        start += width


def _split_w_in(w_in, l):
    _, d, total = w_in.shape
    rows = 256
    out_shape = [jax.ShapeDtypeStruct((d, width), BF16) if name in _W_IN_SPLIT else
                 jax.ShapeDtypeStruct((d, LANE), F32) for name, width in _W_IN_SEGMENTS]
    outs = pl.pallas_call(
        _split_w_in_kernel,
        grid=(d // rows,),
        in_specs=[pl.BlockSpec((None, rows, total), lambda i: (l, i, 0))],
        out_specs=[_row_spec(rows, o.shape[1]) for o in out_shape],
        out_shape=out_shape,
        compiler_params=_params("arbitrary"),
        name="split_w_in",
    )(w_in)
    return dict(zip((name for name, _ in _W_IN_SEGMENTS), outs))


def _layer_weights(l, g_mix, w_in, g_q, g_kv, w_uq, w_uk, w_uv, g_sg, w_s, b_s, w_br,
                   b_gate, w_out):
    split = _split_w_in(w_in, l)
    w_kr = split.pop("w_kr")[:, :MLA_ROPE]
    rope_pad = ((0, 0), (MLA_NOPE, HEAD_PAD - MLA_QK))
    w_kr2 = jnp.concatenate([jnp.pad(w_kr, rope_pad), jnp.pad(_rotate_half_cols(w_kr), rope_pad)],
                            axis=1)
    uq = w_uq[l].reshape(Q_RANK, MLA_HEADS, MLA_QK)
    uq_rot = _rotate_half_cols(uq[..., MLA_NOPE:]).reshape(Q_RANK, MLA_HEADS * MLA_ROPE)
    w_br0 = jnp.pad(w_br[l, 0].reshape(MLA_HEADS, MLA_V, D_MODEL),
                    ((0, 0), (0, HEAD_PAD - MLA_V), (0, 0))).reshape(QK_PAD, D_MODEL)
    return {
        "g_mix": g_mix[l][None, :],
        **split,
        "w_kr": w_kr2.astype(BF16),
        "g_q": g_q[l][None, :],
        "g_kv": g_kv[l][None, :],
        "w_uq_t": _pad_heads(w_uq[l], MLA_QK).T.astype(BF16),
        "w_uqr_t": uq_rot.T.astype(BF16),
        "w_uk": _pad_heads(w_uk[l], MLA_NOPE).astype(BF16),
        "w_uv_t": _pad_heads(w_uv[l], MLA_V, V_ROWS).T.astype(BF16),
        "g_sg": g_sg[l][None, :],
        "w_s": w_s[l].astype(BF16),
        "b_st": b_s[l].T,
        "b_gate": b_gate[l],
        "w_br0": w_br0.astype(BF16),
        "w_br1": w_br[l, 1].astype(BF16),
        "w_br2": w_br[l, 2].astype(BF16),
        "w_out": w_out[l].astype(BF16),
    }


def _rope_constants():
    half = MLA_ROPE // 2
    inv_freq = 1.0 / (ROPE_THETA ** (jnp.arange(0, MLA_ROPE, 2, dtype=F32) / MLA_ROPE))
    lane_freq = jnp.zeros((HEAD_PAD,), F32)
    lane_freq = lane_freq.at[MLA_NOPE:MLA_NOPE + half].set(inv_freq)
    lane_freq = lane_freq.at[MLA_NOPE + half:MLA_QK].set(inv_freq)
    v_one = jnp.zeros((MLA_HEADS, V_ROWS), F32).at[:, MLA_V].set(1.0).reshape(1, MLA_HEADS * V_ROWS)
    return lane_freq[None, :], v_one


def _tiles(s):
    tm = min(512, s)
    tq = min(512, s)
    tk = min(512, s)
    tmoe = min(512, s)
    return tm, tq, tk, tmoe


def kernel(x, mem, positions, g_mix, w_in, g_q, g_kv, w_uq, w_uk, w_uv, g_sg, w_s, b_s, g_mem, w_mkv, w_br, b_gate, w_out, g_ffn, dense_w1, dense_w3, dense_w2, w_router, moe_w1, moe_w3, moe_w2, g_final):
    b, s, d = x.shape
    assert b == 1 and d == D_MODEL and s % SG_CHUNK == 0
    depth = g_mix.shape[0]
    tm, tq, tk, tmoe = _tiles(s)
    xs = x[0]
    pos = positions[0].astype(F32)[:, None]
    inv_freq, v_one = _rope_constants()
    rope = _rope_tables(pos, inv_freq, tm)
    moe_bf16 = [w.astype(BF16) for w in (moe_w1, moe_w3, moe_w2)]
    for l in range(depth):
        lw = _layer_weights(l, g_mix, w_in, g_q, g_kv, w_uq, w_uk, w_uv, g_sg, w_s, b_s, w_br,
                            b_gate, w_out)
        kv_mem = _mem_kv(mem[0], g_mem[None, :], w_mkv[l].astype(BF16))
        lw["k_mem_t"] = kv_mem[:, :BRANCH_WIDTH].T
        lw["v_mem"] = kv_mem[:, BRANCH_WIDTH:]
        lw["v_one"] = v_one
        qt, k, vt, yb, yc = _mixer_pre(xs, rope, lw, tm)
        ya = _flash(qt, k, vt, tq, tk)
        xs = _mixer_post(xs, ya, yb, yc, lw, tm)
        j = l // 2
        if l % 2 == 0:
            xs = _dense_ffn(xs, g_ffn[l][None, :], dense_w1[j].astype(BF16),
                            dense_w3[j].astype(BF16), dense_w2[j].astype(BF16), tm)
        else:
            fc = moe_w1.shape[-1] // 2
            xs = _moe(xs, g_ffn[l][None, :], w_router[j], *moe_bf16, j, tmoe, fc)
    return _final_norm(xs, g_final[None, :], tm)[None]
```

```python
import functools

import jax
import jax.numpy as jnp
from jax import lax
from jax.experimental import pallas as pl
from jax.experimental.pallas import tpu as pltpu

D_MODEL = 1024
N_BRANCH = 3
BRANCH_WIDTH = 512
MLA_HEADS = 8
MLA_NOPE = 64
MLA_ROPE = 32
MLA_QK = MLA_NOPE + MLA_ROPE
MLA_V = BRANCH_WIDTH // MLA_HEADS
Q_RANK = 384
KV_RANK = 256
ROPE_THETA = 10000.0
SG_CHUNK = 128
SG_GROUPS = 4
SG_WIDTH = BRANCH_WIDTH
X_HEADS = 4
X_HEAD_DIM = BRANCH_WIDTH // X_HEADS
N_EXPERTS = 8
EPS = 1e-6
LOG2_E = 1.4426950408889634

LANE = 128
HEAD_PAD = LANE
QK_PAD = MLA_HEADS * HEAD_PAD
V_ROWS = 80
VMEM_LIMIT = 56 * 1024 * 1024
FLASH_QBLOCKS = 4
FLASH_AHEAD = 3
FLASH_UNROLL = 12

F32 = jnp.float32
BF16 = jnp.bfloat16


def _dot(a, b):
    return jnp.dot(a, b, preferred_element_type=F32)


def _dot_nt(a, b):
    return lax.dot_general(a, b, (((1,), (1,)), ((), ())), preferred_element_type=F32)


def _rms(x, g):
    return x * lax.rsqrt(jnp.mean(x * x, axis=-1, keepdims=True) + EPS) * g


def _const_spec(shape):
    nd = len(shape)
    return pl.BlockSpec(shape, lambda *_: (0,) * nd, pipeline_mode=pl.Buffered(1))


def _row_spec(tm, width):
    return pl.BlockSpec((tm, width), lambda i: (i, 0))


def _params(*sem):
    return pltpu.CompilerParams(dimension_semantics=sem, vmem_limit_bytes=VMEM_LIMIT)


def _mem_kv_kernel(mem_ref, g_ref, w_ref, o_ref):
    mem_n = _rms(mem_ref[...], g_ref[...]).astype(BF16)
    o_ref[...] = _dot(mem_n, w_ref[...]).astype(BF16)


def _mem_kv(mem, g_mem, w_mkv):
    m = mem.shape[0]
    return pl.pallas_call(
        _mem_kv_kernel,
        out_shape=jax.ShapeDtypeStruct((m, 2 * BRANCH_WIDTH), BF16),
        name="mem_kv",
    )(mem, g_mem, w_mkv)


def _rope_tables_kernel(pos_ref, posr_ref, invf_ref, invfc_ref, cs_out, sn_out, cst_out, snt_out):
    ang = pos_ref[...] * invf_ref[...]
    cs_out[...] = jnp.cos(ang)
    sn_out[...] = jnp.sin(ang)
    ang_t = invfc_ref[...] * posr_ref[...]
    cst_out[...] = jnp.cos(ang_t)
    snt_out[...] = jnp.sin(ang_t)


def _rope_tables(pos, inv_freq, tm):
    s = pos.shape[0]
    row = jax.ShapeDtypeStruct((s, HEAD_PAD), F32)
    col = jax.ShapeDtypeStruct((HEAD_PAD, s), F32)
    return pl.pallas_call(
        _rope_tables_kernel,
        grid=(s // tm,),
        in_specs=[_row_spec(tm, 1), pl.BlockSpec((1, tm), lambda i: (0, i)),
                  _const_spec(inv_freq.shape), _const_spec(inv_freq.T.shape)],
        out_specs=[_row_spec(tm, HEAD_PAD)] * 2 + [pl.BlockSpec((HEAD_PAD, tm), lambda i: (0, i))] * 2,
        out_shape=[row, row, col, col],
        compiler_params=_params("arbitrary"),
        name="rope_tables",
    )(pos, pos.reshape(1, s), inv_freq, inv_freq.T)


def _mixer_pre_kernel(x_ref, cs_ref, sn_ref, cst_ref, snt_ref, gmix_ref, wdq_ref, wdkv_ref,
                      wkr_ref, wsg_ref, wxq_ref, gq_ref, gkv_ref, wuqt_ref, wuqrt_ref, wuk_ref,
                      wuvt_ref, vonec_ref, gsg_ref, ws_ref, bst_ref, kmt_ref, vm_ref,
                      qt_out, k_out, vt_out, yb_out, yc_out):
    tm = x_ref.shape[0]
    h = _rms(x_ref[...], gmix_ref[...]).astype(BF16)

    cq = _rms(_dot(h, wdq_ref[...]), gq_ref[...]).astype(BF16)
    qa_t = _dot_nt(wuqt_ref[...], cq)
    qb_t = _dot_nt(wuqrt_ref[...], cq)
    ckv = _rms(_dot(h, wdkv_ref[...]), gkv_ref[...]).astype(BF16)
    kn = _dot(ckv, wuk_ref[...])
    kr = _dot(h, wkr_ref[...])
    k_rope = kr[:, :LANE] * cs_ref[...] + kr[:, LANE:] * sn_ref[...]
    scale = MLA_QK ** -0.5 * LOG2_E
    cs_r = cst_ref[MLA_NOPE:MLA_QK, :]
    sn_r = snt_ref[MLA_NOPE:MLA_QK, :]
    for hd in range(MLA_HEADS):
        sl = slice(hd * HEAD_PAD, (hd + 1) * HEAD_PAD)
        qa = qa_t[sl, :]
        rope = qa[MLA_NOPE:MLA_QK, :] * cs_r + qb_t[hd * MLA_ROPE:(hd + 1) * MLA_ROPE, :] * sn_r
        q_head = jnp.concatenate([qa[:MLA_NOPE, :], rope, qa[MLA_QK:, :]], axis=0)
        qt_out[sl, :] = (q_head * scale).astype(BF16)
        k_out[:, sl] = (kn[:, sl] + k_rope).astype(BF16)
    vt_out[...] = (_dot_nt(wuvt_ref[...], ckv) + vonec_ref[...]).astype(BF16)

    z = jax.nn.gelu(_dot(h, wsg_ref[...]))
    u = z[:, :SG_WIDTH]
    vn = _rms(z[:, SG_WIDTH:], gsg_ref[...]).astype(BF16)
    gw = SG_WIDTH // SG_GROUPS
    for n in range(tm // SG_CHUNK):
        rows = slice(n * SG_CHUNK, (n + 1) * SG_CHUNK)
        for g in range(SG_GROUPS):
            cols = slice(g * gw, (g + 1) * gw)
            sg = _dot(ws_ref[g], vn[rows, cols]) + bst_ref[:, g:g + 1]
            yb_out[rows, cols] = (u[rows, cols] * sg).astype(BF16)

    qx = _dot(h, wxq_ref[...]) * (X_HEAD_DIM ** -0.5)
    for hd in range(X_HEADS):
        sl = slice(hd * X_HEAD_DIM, (hd + 1) * X_HEAD_DIM)
        sc = _dot(qx[:, sl].astype(BF16), kmt_ref[sl, :])
        p = jnp.exp(sc - jnp.max(sc, axis=-1, keepdims=True))
        l = jnp.sum(p, axis=-1, keepdims=True)
        o = _dot(p.astype(BF16), vm_ref[:, sl])
        yc_out[:, sl] = (o / l).astype(BF16)


def _col_spec(height, tm):
    return pl.BlockSpec((height, tm), lambda i: (0, i))


def _mixer_pre(x, rope, lw, tm):
    s = x.shape[0]
    ins = [x, *rope, lw["g_mix"], lw["w_dq"], lw["w_dkv"], lw["w_kr"], lw["w_sg"],
           lw["w_xq"], lw["g_q"], lw["g_kv"], lw["w_uq_t"], lw["w_uqr_t"], lw["w_uk"],
           lw["w_uv_t"], lw["v_one"].T, lw["g_sg"], lw["w_s"],
           lw["b_st"], lw["k_mem_t"], lw["v_mem"]]
    in_specs = [_row_spec(tm, D_MODEL), _row_spec(tm, HEAD_PAD), _row_spec(tm, HEAD_PAD),
                _col_spec(HEAD_PAD, tm), _col_spec(HEAD_PAD, tm)] + \
               [_const_spec(a.shape) for a in ins[5:]]
    v_rows = MLA_HEADS * V_ROWS
    out_shape = [jax.ShapeDtypeStruct((QK_PAD, s), BF16), jax.ShapeDtypeStruct((s, QK_PAD), BF16),
                 jax.ShapeDtypeStruct((v_rows, s), BF16)] + \
                [jax.ShapeDtypeStruct((s, BRANCH_WIDTH), BF16)] * 2
    out_specs = [_col_spec(QK_PAD, tm), _row_spec(tm, QK_PAD), _col_spec(v_rows, tm)] + \
                [_row_spec(tm, BRANCH_WIDTH)] * 2
    return pl.pallas_call(
        _mixer_pre_kernel,
        grid=(s // tm,),
        in_specs=in_specs,
        out_specs=out_specs,
        out_shape=out_shape,
        compiler_params=_params("arbitrary"),
        name="mixer_pre",
    )(*ins)


def _flash_kernel(qt_ref, k_ref, vt_ref, o_ref, s_scr, m_scr, acc_scr, *, tq, tk):
    n = k_ref.shape[0] // tk
    groups = qt_ref.shape[1] // tq

    def scores(g, t, slot):
        start = pl.multiple_of(t * tk, tk)
        s_scr[slot] = _dot(k_ref[pl.ds(start, tk), :], qt_ref[:, g * tq:(g + 1) * tq])

    def update(g, t, slot):
        start = pl.multiple_of(t * tk, tk)
        sc = s_scr[slot]
        m = m_scr[g]
        m_new = jnp.maximum(m, jnp.max(sc, axis=0, keepdims=True))
        alpha = jnp.exp2(m - m_new)
        p = jnp.exp2(sc - m_new).astype(BF16)
        acc_scr[g] = acc_scr[g] * alpha + _dot(vt_ref[:, pl.ds(start, tk)], p)
        m_scr[g] = m_new

    slots = FLASH_AHEAD + 1

    def scores_at(pos):
        g, t = divmod(pos, n)
        if g < groups:
            scores(g, t, pos % slots)

    def step(g, t):
        scores_at(g * n + t + FLASH_AHEAD)
        update(g, t, (g * n + t) % slots)

    def finish(g):
        acc = acc_scr[g]
        out_t = acc / acc[MLA_V:MLA_V + 1, :]
        pad = jnp.zeros((HEAD_PAD - V_ROWS, tq), F32)
        o_ref[g * tq:(g + 1) * tq, :] = jnp.concatenate([out_t, pad], axis=0).T.astype(BF16)

    m_scr[...] = jnp.full(m_scr.shape, -jnp.inf, F32)
    acc_scr[...] = jnp.zeros(acc_scr.shape, F32)
    for pos in range(FLASH_AHEAD):
        scores_at(pos)
    loops = max(n - 1 - FLASH_AHEAD, 0) // FLASH_UNROLL
    for g in range(groups):
        off = g * n
        step(g, 0)
        if g > 0:
            finish(g - 1)

        def body(u, carry, g=g, off=off):
            base = 1 + u * FLASH_UNROLL
            for r in range(FLASH_UNROLL):
                scores(g, base + r + FLASH_AHEAD, (off + 1 + r + FLASH_AHEAD) % slots)
                update(g, base + r, (off + 1 + r) % slots)
            return carry

        lax.fori_loop(0, loops, body, 0)
        for t in range(1 + loops * FLASH_UNROLL, n):
            step(g, t)
    finish(groups - 1)


def _flash(qt, k, vt, tq, tk):
    s = k.shape[0]
    groups = FLASH_QBLOCKS if s % (FLASH_QBLOCKS * tq) == 0 else 1
    tqg = groups * tq
    return pl.pallas_call(
        functools.partial(_flash_kernel, tq=tq, tk=tk),
        grid=(MLA_HEADS, s // tqg),
        in_specs=[pl.BlockSpec((HEAD_PAD, tqg), lambda h, i: (h, i)),
                  pl.BlockSpec((s, HEAD_PAD), lambda h, i: (0, h)),
                  pl.BlockSpec((V_ROWS, s), lambda h, i: (h, 0))],
        out_specs=pl.BlockSpec((tqg, HEAD_PAD), lambda h, i: (i, h)),
        out_shape=jax.ShapeDtypeStruct((s, QK_PAD), BF16),
        scratch_shapes=[pltpu.VMEM((FLASH_AHEAD + 1, tk, tq), F32),
                        pltpu.VMEM((groups, 1, tq), F32),
                        pltpu.VMEM((groups, V_ROWS, tq), F32)],
        compiler_params=_params("arbitrary", "arbitrary"),
        name="flash",
    )(qt, k, vt)


def _mixer_post_kernel(x_ref, ya_ref, yb_ref, yc_ref, gmix_ref, wgate_ref, bgate_ref,
                       wbr0_ref, wbr1_ref, wbr2_ref, wout_ref, o_ref):
    x = x_ref[...]
    h = _rms(x, gmix_ref[...]).astype(BF16)
    mix = None
    for n, (y_ref, w_ref) in enumerate(((ya_ref, wbr0_ref), (yb_ref, wbr1_ref), (yc_ref, wbr2_ref))):
        cols = slice(n * D_MODEL, (n + 1) * D_MODEL)
        gate = jax.nn.sigmoid(_dot(h, wgate_ref[:, cols]) + bgate_ref[n:n + 1, :])
        term = gate * _dot(y_ref[...], w_ref[...])
        mix = term if mix is None else mix + term
    o_ref[...] = x + _dot(mix.astype(BF16), wout_ref[...])


def _mixer_post(x, ya, yb, yc, lw, tm):
    s = x.shape[0]
    consts = [lw["g_mix"], lw["w_gate"], lw["b_gate"], lw["w_br0"], lw["w_br1"], lw["w_br2"],
              lw["w_out"]]
    in_specs = [_row_spec(tm, D_MODEL), _row_spec(tm, QK_PAD), _row_spec(tm, BRANCH_WIDTH),
                _row_spec(tm, BRANCH_WIDTH)] + [_const_spec(a.shape) for a in consts]
    return pl.pallas_call(
        _mixer_post_kernel,
        grid=(s // tm,),
        in_specs=in_specs,
        out_specs=_row_spec(tm, D_MODEL),
        out_shape=jax.ShapeDtypeStruct((s, D_MODEL), F32),
        compiler_params=_params("arbitrary"),
        name="mixer_post",
    )(x, ya, yb, yc, *consts)


def _dense_ffn_kernel(x_ref, g_ref, w1_ref, w3_ref, w2_ref, o_ref):
    x = x_ref[...]
    h = _rms(x, g_ref[...]).astype(BF16)
    t = jax.nn.silu(_dot(h, w1_ref[...])) * _dot(h, w3_ref[...])
    o_ref[...] = x + _dot(t.astype(BF16), w2_ref[...])


def _dense_ffn(x, g, w1, w3, w2, tm):
    s = x.shape[0]
    consts = [g, w1, w3, w2]
    return pl.pallas_call(
        _dense_ffn_kernel,
        grid=(s // tm,),
        in_specs=[_row_spec(tm, D_MODEL)] + [_const_spec(a.shape) for a in consts],
        out_specs=_row_spec(tm, D_MODEL),
        out_shape=jax.ShapeDtypeStruct((s, D_MODEL), F32),
        compiler_params=_params("arbitrary"),
        name="dense_ffn",
    )(x, *consts)


def _route_kernel(x_ref, g_ref, wrt_ref, tri_ref, h_out, e_out, rank_out, wcol_out, cnt_out,
                  carry_scr):
    tm = x_ref.shape[0]

    @pl.when(pl.program_id(0) == 0)
    def _init():
        carry_scr[...] = jnp.zeros(carry_scr.shape, F32)

    h = _rms(x_ref[...], g_ref[...])
    h_out[...] = h
    logits = _dot_nt(wrt_ref[...], h.astype(BF16))[:N_EXPERTS, :]
    row = lax.broadcasted_iota(jnp.int32, logits.shape, 0)
    m1 = jnp.max(logits, axis=0, keepdims=True)
    i1 = jnp.min(jnp.where(logits == m1, row, N_EXPERTS), axis=0, keepdims=True)
    rest = jnp.where(row == i1, -jnp.inf, logits)
    m2 = jnp.max(rest, axis=0, keepdims=True)
    i2 = jnp.min(jnp.where(rest == m2, row, N_EXPERTS), axis=0, keepdims=True)
    e2 = jnp.exp(m2 - m1)
    den = 1.0 + e2
    oh1 = (row == i1).astype(F32)
    oh2 = (row == i2).astype(F32)
    cum = _dot(jnp.concatenate([oh1, oh2], axis=0).astype(BF16), tri_ref[...])
    tot = cum[:N_EXPERTS] + cum[N_EXPERTS:] + carry_scr[:, 0:1]
    rank1 = jnp.sum(oh1 * tot, axis=0, keepdims=True) - 1.0
    rank2 = jnp.sum(oh2 * tot, axis=0, keepdims=True) - 1.0
    e_out[...] = jnp.concatenate([i1, i2], axis=0)
    rank_out[...] = jnp.concatenate([rank1, rank2], axis=0).astype(jnp.int32)
    carry_scr[...] = jnp.broadcast_to(tot[:, tm - 1:tm], carry_scr.shape)
    cnt_out[...] = carry_scr[...]
    w_rows = jnp.concatenate([1.0 / den, e2 / den, jnp.zeros((LANE - 2, tm), F32)], axis=0)
    wcol_out[...] = w_rows.T


def _route(x, g, w_router_t, tm):
    s = x.shape[0]
    tri = (jnp.arange(tm)[:, None] <= jnp.arange(tm)[None, :]).astype(BF16)
    pair = lambda dt: jax.ShapeDtypeStruct((2, s), dt)
    return pl.pallas_call(
        _route_kernel,
        grid=(s // tm,),
        in_specs=[_row_spec(tm, D_MODEL), _const_spec(g.shape), _const_spec(w_router_t.shape),
                  _const_spec(tri.shape)],
        out_specs=[_row_spec(tm, D_MODEL), _col_spec(2, tm), _col_spec(2, tm), _row_spec(tm, LANE),
                   pl.BlockSpec((N_EXPERTS, LANE), lambda i: (0, 0))],
        out_shape=[jax.ShapeDtypeStruct((s, D_MODEL), F32), pair(jnp.int32), pair(jnp.int32),
                   jax.ShapeDtypeStruct((s, LANE), F32),
                   jax.ShapeDtypeStruct((N_EXPERTS, LANE), F32)],
        scratch_shapes=[pltpu.VMEM((N_EXPERTS, LANE), F32)],
        compiler_params=_params("arbitrary"),
        name="moe_route",
    )(x, g, w_router_t, tri)


def _row_copy(src_ref, src_row, dst_ref, dst_row, sem):
    return pltpu.make_async_copy(src_ref.at[pl.ds(src_row, 1)], dst_ref.at[pl.ds(dst_row, 1)], sem)


def _dispatch_kernel(pos_ref, h_ref, xs_in_ref, xs_ref, sem):
    del xs_in_ref
    tm = h_ref.shape[0]

    def issue(t, carry):
        for k in range(2):
            _row_copy(h_ref, t, xs_ref, pos_ref[k, t], sem).start()
        return carry

    lax.fori_loop(0, tm, issue, 0, unroll=8)
    for k in range(2):
        pltpu.make_async_copy(h_ref, xs_ref.at[pl.ds(0, tm)], sem).wait()


def _dispatch(pos, h, n_slots, tm):
    s = h.shape[0]
    xs0 = jnp.zeros((n_slots, D_MODEL), F32)
    return pl.pallas_call(
        _dispatch_kernel,
        grid=(s // tm,),
        in_specs=[pl.BlockSpec((2, tm), lambda i: (0, i), memory_space=pltpu.SMEM),
                  _row_spec(tm, D_MODEL), pl.BlockSpec(memory_space=pl.ANY)],
        out_specs=pl.BlockSpec(memory_space=pl.ANY),
        out_shape=jax.ShapeDtypeStruct((n_slots, D_MODEL), F32),
        scratch_shapes=[pltpu.SemaphoreType.DMA(())],
        input_output_aliases={2: 0},
        compiler_params=pltpu.CompilerParams(dimension_semantics=("arbitrary",),
                                             vmem_limit_bytes=VMEM_LIMIT, has_side_effects=True),
        name="moe_dispatch",
    )(pos, h, xs0)


def _expert_kernel(te_ref, nu_ref, xs_ref, w1_ref, w3_ref, w2_ref, ys_ref, h_scr):
    del te_ref
    c = pl.program_id(1)

    @pl.when(c == 0)
    def _load():
        h_scr[...] = xs_ref[...].astype(BF16)
        ys_ref[...] = jnp.zeros(ys_ref.shape, F32)

    @pl.when(pl.program_id(0) < nu_ref[0])
    def _ffn():
        h = h_scr[...]
        t = jax.nn.silu(_dot(h, w1_ref[...])) * _dot(h, w3_ref[...])
        ys_ref[...] += _dot(t.astype(BF16), w2_ref[...])


def _experts(tile_expert, n_used, xs, w1, w3, w2, j, tm, fc):
    n_slots = xs.shape[0]
    n_chunks = w1.shape[-1] // fc

    def chunk(i, c, nu):
        return jnp.where(i < nu[0], c, n_chunks - 1)

    grid_spec = pltpu.PrefetchScalarGridSpec(
        num_scalar_prefetch=2,
        grid=(n_slots // tm, n_chunks),
        in_specs=[
            pl.BlockSpec((tm, D_MODEL), lambda i, c, te, nu: (i, 0)),
            pl.BlockSpec((None, None, D_MODEL, fc),
                         lambda i, c, te, nu: (j, te[i], 0, chunk(i, c, nu))),
            pl.BlockSpec((None, None, D_MODEL, fc),
                         lambda i, c, te, nu: (j, te[i], 0, chunk(i, c, nu))),
            pl.BlockSpec((None, None, fc, D_MODEL),
                         lambda i, c, te, nu: (j, te[i], chunk(i, c, nu), 0)),
        ],
        out_specs=pl.BlockSpec((tm, D_MODEL), lambda i, c, te, nu: (i, 0)),
        scratch_shapes=[pltpu.VMEM((tm, D_MODEL), BF16)],
    )
    return pl.pallas_call(
        _expert_kernel,
        grid_spec=grid_spec,
        out_shape=jax.ShapeDtypeStruct((n_slots, D_MODEL), F32),
        compiler_params=_params("arbitrary", "arbitrary"),
        name="moe_experts",
    )(tile_expert, n_used, xs, w1, w3, w2)


def _combine_kernel(pos_ref, x_ref, wcol_ref, ys_ref, o_ref, buf, sem):
    tm = x_ref.shape[0]

    def issue(t, carry):
        for k in range(2):
            _row_copy(ys_ref, pos_ref[k, t], buf.at[k], t, sem).start()
        return carry

    lax.fori_loop(0, tm, issue, 0, unroll=8)
    for k in range(2):
        pltpu.make_async_copy(ys_ref.at[pl.ds(0, tm)], buf.at[k], sem).wait()
    w = wcol_ref[...]
    o_ref[...] = x_ref[...] + w[:, 0:1] * buf[0] + w[:, 1:2] * buf[1]


def _combine(pos, x, wcol, ys, tm):
    s = x.shape[0]
    return pl.pallas_call(
        _combine_kernel,
        grid=(s // tm,),
        in_specs=[pl.BlockSpec((2, tm), lambda i: (0, i), memory_space=pltpu.SMEM),
                  _row_spec(tm, D_MODEL), _row_spec(tm, LANE), pl.BlockSpec(memory_space=pl.ANY)],
        out_specs=_row_spec(tm, D_MODEL),
        out_shape=jax.ShapeDtypeStruct((s, D_MODEL), F32),
        scratch_shapes=[pltpu.VMEM((2, tm, D_MODEL), F32), pltpu.SemaphoreType.DMA(())],
        compiler_params=_params("arbitrary"),
        name="moe_combine",
    )(pos, x, wcol, ys)


def _moe(x, g, w_router, w1, w3, w2, j, tm, fc):
    s = x.shape[0]
    n_tiles = 2 * s // tm + N_EXPERTS
    wrt = jnp.pad(w_router.T, ((0, 16 - N_EXPERTS), (0, 0))).astype(BF16)
    h, e, rank, wcol, cnt = _route(x, g, wrt, tm)
    counts = cnt[:, 0].astype(jnp.int32)
    padded = (counts + tm - 1) // tm * tm
    ends = jnp.cumsum(padded)
    offs = ends - padded
    pos = rank
    for ex in range(N_EXPERTS):
        pos = pos + jnp.where(e == ex, offs[ex], 0)
    tile_start = jnp.arange(n_tiles, dtype=jnp.int32) * tm
    tile_expert = jnp.minimum(jnp.sum(tile_start[:, None] >= ends[None, :], axis=1),
                              N_EXPERTS - 1).astype(jnp.int32)
    n_used = (ends[-1:] // tm).astype(jnp.int32)
    xs = _dispatch(pos, h, n_tiles * tm, tm)
    ys = _experts(tile_expert, n_used, xs, w1, w3, w2, j, tm, fc)
    return _combine(pos, x, wcol, ys, tm)


def _final_norm_kernel(x_ref, g_ref, o_ref):
    o_ref[...] = _rms(x_ref[...], g_ref[...])


def _final_norm(x, g, tm):
    s = x.shape[0]
    return pl.pallas_call(
        _final_norm_kernel,
        grid=(s // tm,),
        in_specs=[_row_spec(tm, D_MODEL), _const_spec(g.shape)],
        out_specs=_row_spec(tm, D_MODEL),
        out_shape=jax.ShapeDtypeStruct((s, D_MODEL), F32),
        compiler_params=_params("arbitrary"),
        name="final_norm",
    )(x, g)


def _pad_heads(w, width, padded=HEAD_PAD):
    r = w.shape[0]
    w = w.reshape(r, MLA_HEADS, width)
    return jnp.pad(w, ((0, 0), (0, 0), (0, padded - width))).reshape(r, MLA_HEADS * padded)


def _rotate_half_cols(w_rope):
    half = MLA_ROPE // 2
    return jnp.concatenate([-w_rope[..., half:], w_rope[..., :half]], axis=-1)


_W_IN_SEGMENTS = (("w_dq", Q_RANK), ("w_dkv", KV_RANK), ("w_kr", MLA_ROPE), ("w_sg", 2 * SG_WIDTH),
                  ("w_xq", X_HEADS * X_HEAD_DIM), ("w_gate", N_BRANCH * D_MODEL))
_W_IN_SPLIT = tuple(name for name, _ in _W_IN_SEGMENTS if name != "w_kr")


def _split_w_in_kernel(w_ref, *outs):
    start = 0
    out = iter(outs)
    for name, width in _W_IN_SEGMENTS:
        if name in _W_IN_SPLIT:
            next(out)[...] = w_ref[:, start:start + width].astype(BF16)
        else:
            next(out)[...] = w_ref[:, start:start + LANE]
        start += width


def _split_w_in(w_in, l):
    _, d, total = w_in.shape
    rows = 256
    out_shape = [jax.ShapeDtypeStruct((d, width), BF16) if name in _W_IN_SPLIT else
                 jax.ShapeDtypeStruct((d, LANE), F32) for name, width in _W_IN_SEGMENTS]
    outs = pl.pallas_call(
        _split_w_in_kernel,
        grid=(d // rows,),
        in_specs=[pl.BlockSpec((None, rows, total), lambda i: (l, i, 0))],
        out_specs=[_row_spec(rows, o.shape[1]) for o in out_shape],
        out_shape=out_shape,
        compiler_params=_params("arbitrary"),
        name="split_w_in",
    )(w_in)
    return dict(zip((name for name, _ in _W_IN_SEGMENTS), outs))


def _layer_weights(l, g_mix, w_in, g_q, g_kv, w_uq, w_uk, w_uv, g_sg, w_s, b_s, w_br,
                   b_gate, w_out):
    split = _split_w_in(w_in, l)
    w_kr = split.pop("w_kr")[:, :MLA_ROPE]
    rope_pad = ((0, 0), (MLA_NOPE, HEAD_PAD - MLA_QK))
    w_kr2 = jnp.concatenate([jnp.pad(w_kr, rope_pad), jnp.pad(_rotate_half_cols(w_kr), rope_pad)],
                            axis=1)
    uq = w_uq[l].reshape(Q_RANK, MLA_HEADS, MLA_QK)
    uq_rot = _rotate_half_cols(uq[..., MLA_NOPE:]).reshape(Q_RANK, MLA_HEADS * MLA_ROPE)
    w_br0 = jnp.pad(w_br[l, 0].reshape(MLA_HEADS, MLA_V, D_MODEL),
                    ((0, 0), (0, HEAD_PAD - MLA_V), (0, 0))).reshape(QK_PAD, D_MODEL)
    return {
        "g_mix": g_mix[l][None, :],
        **split,
        "w_kr": w_kr2.astype(BF16),
        "g_q": g_q[l][None, :],
        "g_kv": g_kv[l][None, :],
        "w_uq_t": _pad_heads(w_uq[l], MLA_QK).T.astype(BF16),
        "w_uqr_t": uq_rot.T.astype(BF16),
        "w_uk": _pad_heads(w_uk[l], MLA_NOPE).astype(BF16),
        "w_uv_t": _pad_heads(w_uv[l], MLA_V, V_ROWS).T.astype(BF16),
        "g_sg": g_sg[l][None, :],
        "w_s": w_s[l].astype(BF16),
        "b_st": b_s[l].T,
        "b_gate": b_gate[l],
        "w_br0": w_br0.astype(BF16),
        "w_br1": w_br[l, 1].astype(BF16),
        "w_br2": w_br[l, 2].astype(BF16),
        "w_out": w_out[l].astype(BF16),
    }


def _rope_constants():
    half = MLA_ROPE // 2
    inv_freq = 1.0 / (ROPE_THETA ** (jnp.arange(0, MLA_ROPE, 2, dtype=F32) / MLA_ROPE))
    lane_freq = jnp.zeros((HEAD_PAD,), F32)
    lane_freq = lane_freq.at[MLA_NOPE:MLA_NOPE + half].set(inv_freq)
    lane_freq = lane_freq.at[MLA_NOPE + half:MLA_QK].set(inv_freq)
    v_one = jnp.zeros((MLA_HEADS, V_ROWS), F32).at[:, MLA_V].set(1.0).reshape(1, MLA_HEADS * V_ROWS)
    return lane_freq[None, :], v_one


def _tiles(s):
    tm = min(512, s)
    tq = min(512, s)
    tk = min(512, s)
    tmoe = min(512, s)
    return tm, tq, tk, tmoe


def kernel(x, mem, positions, g_mix, w_in, g_q, g_kv, w_uq, w_uk, w_uv, g_sg, w_s, b_s, g_mem, w_mkv, w_br, b_gate, w_out, g_ffn, dense_w1, dense_w3, dense_w2, w_router, moe_w1, moe_w3, moe_w2, g_final):
    b, s, d = x.shape
    assert b == 1 and d == D_MODEL and s % SG_CHUNK == 0
    depth = g_mix.shape[0]
    tm, tq, tk, tmoe = _tiles(s)
    xs = x[0]
    pos = positions[0].astype(F32)[:, None]
    inv_freq, v_one = _rope_constants()
    rope = _rope_tables(pos, inv_freq, tm)
    moe_bf16 = [w.astype(BF16) for w in (moe_w1, moe_w3, moe_w2)]
    for l in range(depth):
        lw = _layer_weights(l, g_mix, w_in, g_q, g_kv, w_uq, w_uk, w_uv, g_sg, w_s, b_s, w_br,
                            b_gate, w_out)
        kv_mem = _mem_kv(mem[0], g_mem[None, :], w_mkv[l].astype(BF16))
        lw["k_mem_t"] = kv_mem[:, :BRANCH_WIDTH].T
        lw["v_mem"] = kv_mem[:, BRANCH_WIDTH:]
        lw["v_one"] = v_one
        qt, k, vt, yb, yc = _mixer_pre(xs, rope, lw, tm)
        ya = _flash(qt, k, vt, tq, tk)
        xs = _mixer_post(xs, ya, yb, yc, lw, tm)
        j = l // 2
        if l % 2 == 0:
            xs = _dense_ffn(xs, g_ffn[l][None, :], dense_w1[j].astype(BF16),
                            dense_w3[j].astype(BF16), dense_w2[j].astype(BF16), tm)
        else:
            fc = moe_w1.shape[-1] // 2
            xs = _moe(xs, g_ffn[l][None, :], w_router[j], *moe_bf16, j, tmoe, fc)
    return _final_norm(xs, g_final[None, :], tm)[None]
```
